```python
import math
import jax, jax.numpy as jnp
from jax import lax
import numpy as np

D_MODEL = 1024
BATCH = 8
SEQ = 4096
DEPTH = 1

CHUNK = 64
D_MIX = D_MODEL
D_RWKV = D_MIX // 2
D_CONV = D_MIX - D_RWKV
HEAD_SIZE = 64
N_HEADS = D_RWKV // HEAD_SIZE
LORA_W = 64
LORA_A = 64
LORA_G = 128
CONV_WIDTH = 31
D_FF = 2816
D_SHIFT = 3 * D_RWKV + LORA_W + LORA_A + LORA_G
D_IN = D_SHIFT + 2 * D_CONV
RMS_EPS = 1e-6
GN_EPS = 64e-5
LN_EPS = 1e-5
DECAY_SCALE = math.exp(-0.5)

kernel_name = "hybrid_rwkv7_conformer_conv_macaron_block"


def rmsnorm(x, g):
    xf = x.astype(jnp.float32)
    y = xf * lax.rsqrt(jnp.mean(xf * xf, axis=-1, keepdims=True) + RMS_EPS)
    return (y * g.astype(jnp.float32)).astype(x.dtype)


def layernorm(x, g, b):
    xf = x.astype(jnp.float32)
    mu = jnp.mean(xf, axis=-1, keepdims=True)
    var = jnp.mean(jnp.square(xf - mu), axis=-1, keepdims=True)
    y = (xf - mu) * lax.rsqrt(var + LN_EPS)
    return (y * g.astype(jnp.float32) + b.astype(jnp.float32)).astype(x.dtype)


def swiglu(x, w_gu, w_down):
    gu = x @ w_gu
    gate, up = gu[..., :D_FF], gu[..., D_FF:]
    return (jax.nn.silu(gate) * up) @ w_down


def token_shift(y):
    return jnp.pad(y[:, :-1], ((0, 0), (1, 0), (0, 0)))


def rwkv7_recurrence(r, w, k, v, z, b):
    bsz, seq, nh, n = r.shape
    n_chunks = seq // CHUNK

    def to_chunks(t):
        return jnp.transpose(t, (1, 0, 2, 3)).reshape(n_chunks, CHUNK, bsz, nh, n)

    xs = tuple(to_chunks(t) for t in (r, w, k, v, z, b))

    def frame_step(state, inp):
        r_t, w_t, k_t, v_t, z_t, b_t = inp
        sz = jnp.einsum('bhij,bhj->bhi', state, z_t)
        state = (state * w_t[:, :, None, :]
                 + sz[..., None] * b_t[:, :, None, :]
                 + v_t[..., None] * k_t[:, :, None, :])
        y_t = jnp.einsum('bhij,bhj->bhi', state, r_t)
        return state, y_t

    def chunk_step(state, chunk_inp):
        return lax.scan(frame_step, state, chunk_inp)

    state0 = jnp.zeros((bsz, nh, n, n), jnp.float32)
    _, ys = lax.scan(chunk_step, state0, xs)
    ys = ys.reshape(seq, bsz, nh, n)
    return jnp.transpose(ys, (1, 0, 2, 3))


def hybrid_mixer(h, w_in, shift_mu, w_up, w0, a_up, a0, g_up, k_k, k_a, r_k,
                 gn_w, gn_b, conv_dw, conv_b, conv_ln_w, conv_ln_b, w_out):
    bsz, seq, _ = h.shape
    p = h @ w_in
    ps, pc = p[..., :D_SHIFT], p[..., D_SHIFT:]

    ps = ps + (token_shift(ps) - ps) * shift_mu
    o1, o2, o3 = D_RWKV, 2 * D_RWKV, 3 * D_RWKV
    o4, o5 = o3 + LORA_W, o3 + LORA_W + LORA_A
    r, k, v = ps[..., :o1], ps[..., o1:o2], ps[..., o2:o3]
    xw, xa, xg = ps[..., o3:o4], ps[..., o4:o5], ps[..., o5:]

    d = (w0 + jnp.tanh(xw) @ w_up).astype(jnp.float32)
    decay = jnp.exp(-DECAY_SCALE * jax.nn.sigmoid(d))
    a = jax.nn.sigmoid(a0 + xa @ a_up)
    g = jax.nn.sigmoid(xg) @ g_up

    heads = lambda t: t.reshape(bsz, seq, N_HEADS, HEAD_SIZE).astype(jnp.float32)
    kk = heads(k * k_k)
    kk = kk * lax.rsqrt(jnp.maximum(jnp.sum(kk * kk, axis=-1, keepdims=True), 1e-12))
    k = k * (1.0 + (a - 1.0) * k_a)
    rh, kh, vh, ah, wh = heads(r), heads(k), heads(v), heads(a), heads(decay)

    y = rwkv7_recurrence(rh, wh, kh, vh, -kk, kk * ah)
    mu = jnp.mean(y, axis=-1, keepdims=True)
    var = jnp.mean(jnp.square(y - mu), axis=-1, keepdims=True)
    y = (y - mu) * lax.rsqrt(var + GN_EPS)
    y = y * gn_w.astype(jnp.float32).reshape(N_HEADS, HEAD_SIZE) + gn_b.astype(jnp.float32).reshape(N_HEADS, HEAD_SIZE)
    bonus = jnp.sum(rh * kh * r_k.astype(jnp.float32), axis=-1, keepdims=True) * vh
    y = (y + bonus).reshape(bsz, seq, D_RWKV).astype(h.dtype)
    out_a = y * g

    glu = pc[..., :D_CONV] * jax.nn.sigmoid(pc[..., D_CONV:])
    c = lax.conv_general_dilated(
        glu, conv_dw[:, None, :], window_strides=(1,),
        padding=[(CONV_WIDTH - 1, 0)],
        dimension_numbers=('NWC', 'WIO', 'NWC'),
        feature_group_count=D_CONV) + conv_b
    out_b = jax.nn.silu(layernorm(c, conv_ln_w, conv_ln_b))

    return jnp.concatenate([out_a, out_b], axis=-1) @ w_out


def setup_inputs(seed: int = 0) -> dict:
    key = jax.random.key(seed)
    ks = iter(jax.random.split(key, 40))
    f32 = jnp.float32

    def nrm(shape, scale):
        return jax.random.normal(next(ks), shape, f32) * scale

    def gain(shape):
        return 1.0 + nrm(shape, 0.02)

    L = DEPTH
    return {
        "x": nrm((BATCH, SEQ, D_MODEL), 1.0),
        "ffn1_norm_pre": gain((L, D_MODEL)),
        "ffn1_norm_post": gain((L, D_MODEL)),
        "ffn1_w_gu": nrm((L, D_MODEL, 2 * D_FF), D_MODEL ** -0.5),
        "ffn1_w_down": nrm((L, D_FF, D_MODEL), D_FF ** -0.5),
        "mix_norm_pre": gain((L, D_MODEL)),
        "mix_norm_post": gain((L, D_MODEL)),
        "w_in": nrm((L, D_MODEL, D_IN), D_MODEL ** -0.5),
        "shift_mu": jax.random.uniform(next(ks), (L, D_SHIFT), f32, 0.1, 0.9),
        "w_up": nrm((L, LORA_W, D_RWKV), 0.3 * LORA_W ** -0.5),
        "w0": nrm((L, D_RWKV), 0.5),
        "a_up": nrm((L, LORA_A, D_RWKV), 0.3 * LORA_A ** -0.5),
        "a0": nrm((L, D_RWKV), 0.1),
        "g_up": nrm((L, LORA_G, D_RWKV), LORA_G ** -0.5),
        "k_k": 0.85 + nrm((L, D_RWKV), 0.02),
        "k_a": 1.0 + nrm((L, D_RWKV), 0.02),
        "r_k": nrm((L, N_HEADS, HEAD_SIZE), 0.1),
        "gn_w": gain((L, D_RWKV)),
        "gn_b": nrm((L, D_RWKV), 0.01),
        "conv_dw": nrm((L, CONV_WIDTH, D_CONV), CONV_WIDTH ** -0.5),
        "conv_b": nrm((L, D_CONV), 0.01),
        "conv_ln_w": gain((L, D_CONV)),
        "conv_ln_b": nrm((L, D_CONV), 0.01),
        "w_out": nrm((L, D_MIX, D_MODEL), D_MIX ** -0.5),
        "ffn2_norm_pre": gain((L, D_MODEL)),
        "ffn2_norm_post": gain((L, D_MODEL)),
        "ffn2_w_gu": nrm((L, D_MODEL, 2 * D_FF), D_MODEL ** -0.5),
        "ffn2_w_down": nrm((L, D_FF, D_MODEL), D_FF ** -0.5),
    }


def reference(x, ffn1_norm_pre, ffn1_norm_post, ffn1_w_gu, ffn1_w_down,
              mix_norm_pre, mix_norm_post, w_in, shift_mu, w_up, w0, a_up, a0,
              g_up, k_k, k_a, r_k, gn_w, gn_b, conv_dw, conv_b, conv_ln_w,
              conv_ln_b, w_out, ffn2_norm_pre, ffn2_norm_post, ffn2_w_gu,
              ffn2_w_down):
    for l in range(DEPTH):
        f = swiglu(rmsnorm(x, ffn1_norm_pre[l]), ffn1_w_gu[l], ffn1_w_down[l])
        x = x + 0.5 * rmsnorm(f, ffn1_norm_post[l])
        m = hybrid_mixer(rmsnorm(x, mix_norm_pre[l]), w_in[l], shift_mu[l], w_up[l],
                         w0[l], a_up[l], a0[l], g_up[l], k_k[l], k_a[l], r_k[l],
                         gn_w[l], gn_b[l], conv_dw[l], conv_b[l], conv_ln_w[l],
                         conv_ln_b[l], w_out[l])
        x = x + rmsnorm(m, mix_norm_post[l])
        f = swiglu(rmsnorm(x, ffn2_norm_pre[l]), ffn2_w_gu[l], ffn2_w_down[l])
        x = x + 0.5 * rmsnorm(f, ffn2_norm_post[l])
    return x
```

```python
import functools
import math

import jax
import jax.numpy as jnp
from jax import lax
from jax.experimental import pallas as pl
from jax.experimental.pallas import tpu as pltpu

F32 = jnp.float32
BF16 = jnp.bfloat16

D_MODEL = 1024
D_FF = 2816
D_RWKV = 512
D_CONV = 512
HEAD = 64
LORA_W = 64
LORA_A = 64
LORA_G = 128
D_LORA = LORA_W + LORA_A + LORA_G
D_SHIFT = 3 * D_RWKV + D_LORA
D_IN = D_SHIFT + 2 * D_CONV
CONV_WIDTH = 31
RMS_EPS = 1e-6
GN_EPS = 64e-5
LN_EPS = 1e-5
DECAY_SCALE = math.exp(-0.5)

CHUNK = 64
GROUP = 256
N_GROUPS = D_RWKV // GROUP
HEADS_PER_GROUP = GROUP // HEAD
MID = CHUNK // 2 - 1
CONV_HIST = 32

FFN_TM = 512
FFN_FC = 256
VMEM_LIMIT = 56 * 1024 * 1024

(V_W0, V_A0, V_KK, V_KA, V_RK, V_GNW, V_GNB, V_CB, V_LNW, V_LNB) = range(10)
N_VEC = 16


def _rmsnorm(x, g):
    ms = jnp.mean(x * x, axis=-1, keepdims=True)
    return x * lax.rsqrt(ms + RMS_EPS) * g


def _dot(a, b):
    return jnp.dot(a, b, preferred_element_type=F32)


def _dot_nt(a, b):
    return lax.dot_general(a, b, (((1,), (1,)), ((), ())), preferred_element_type=F32)


def _dot_tn(a, b):
    return lax.dot_general(a, b, (((0,), (0,)), ((), ())), preferred_element_type=F32)


def _ffn_kernel(x_ref, gpre_ref, gpost_ref, wgu_ref, wd_ref, o_ref, act_ref):
    x = x_ref[...]
    h = _rmsnorm(x, gpre_ref[...]).astype(BF16)
    for c in range(D_FF // FFN_FC):
        lo = c * FFN_FC
        gate = _dot(h, wgu_ref[:, lo:lo + FFN_FC])
        up = _dot(h, wgu_ref[:, D_FF + lo:D_FF + lo + FFN_FC])
        act_ref[:, lo:lo + FFN_FC] = (jax.nn.silu(gate) * up).astype(BF16)
    f = _dot(act_ref[...], wd_ref[...])
    o_ref[...] = x + 0.5 * _rmsnorm(f, gpost_ref[...])


def _ffn(x2d, g_pre, g_post, w_gu, w_down):
    n_tok = x2d.shape[0]
    const = lambda i: (0, 0)
    return pl.pallas_call(
        _ffn_kernel,
        out_shape=jax.ShapeDtypeStruct((n_tok, D_MODEL), F32),
        grid=(n_tok // FFN_TM,),
        in_specs=[
            pl.BlockSpec((FFN_TM, D_MODEL), lambda i: (i, 0)),
            pl.BlockSpec((1, D_MODEL), const),
            pl.BlockSpec((1, D_MODEL), const),
            pl.BlockSpec((D_MODEL, 2 * D_FF), const, pipeline_mode=pl.Buffered(1)),
            pl.BlockSpec((D_FF, D_MODEL), const, pipeline_mode=pl.Buffered(1)),
        ],
        out_specs=pl.BlockSpec((FFN_TM, D_MODEL), lambda i: (i, 0)),
        scratch_shapes=[pltpu.VMEM((FFN_TM, D_FF), BF16)],
        compiler_params=pltpu.CompilerParams(
            dimension_semantics=("arbitrary",), vmem_limit_bytes=VMEM_LIMIT),
        name="ffn",
    )(x2d, g_pre, g_post, w_gu, w_down)


def _split3(x):
    hi = x.astype(BF16)
    r1 = x - hi.astype(F32)
    mid = r1.astype(BF16)
    lo = (r1 - mid.astype(F32)).astype(BF16)
    return hi, mid, lo


def _headsum(x, e_ref):
    hi = x.astype(BF16)
    lo = (x - hi.astype(F32)).astype(BF16)
    e = e_ref[...]
    return _dot(hi, e) + _dot(lo, e)


def _mixer_kernel(x_ref, gpre_ref, gpost_ref, win_ref, mu_ref, wlora_ref, vec_ref,
                  dw_ref, e_ref, wout_ref, o_ref,
                  carry_ref, gh_ref, state_ref,
                  r_s, k_s, v_s, z_s, b_s, lw_s, y_s, *, n_batch):
    step = pl.program_id(0)
    m_rows = n_batch * CHUNK

    @pl.when(step == 0)
    def _init():
        carry_ref[...] = jnp.zeros_like(carry_ref)
        gh_ref[:, 0:CONV_HIST, :] = jnp.zeros((n_batch, CONV_HIST, D_CONV), F32)
        state_ref[...] = jnp.zeros_like(state_ref)

    vec = lambda row: vec_ref[row:row + 1, :]

    x = x_ref[...].reshape(m_rows, D_MODEL)
    h = _rmsnorm(x, gpre_ref[...]).astype(BF16)
    p = _dot(h, win_ref[...])

    ps = p[:, :D_SHIFT].reshape(n_batch, CHUNK, D_SHIFT)
    prev = pltpu.roll(ps, 1, axis=1)
    frame = lax.broadcasted_iota(jnp.int32, (1, CHUNK, 1), 1)
    prev = jnp.where(frame == 0, carry_ref[...], prev)
    carry_ref[...] = ps[:, CHUNK - 1:CHUNK, :]
    ps = (ps + (prev - ps) * mu_ref[...]).reshape(m_rows, D_SHIFT)

    r = ps[:, 0:D_RWKV]
    k = ps[:, D_RWKV:2 * D_RWKV]
    v = ps[:, 2 * D_RWKV:3 * D_RWKV]
    xl = ps[:, 3 * D_RWKV:D_SHIFT]

    lane = lax.broadcasted_iota(jnp.int32, (1, D_LORA), 1)
    lact = jnp.where(lane < LORA_W, jnp.tanh(xl),
                     jnp.where(lane < LORA_W + LORA_A, xl, jax.nn.sigmoid(xl)))
    lora = _dot(lact.astype(BF16), wlora_ref[...])
    lw = -DECAY_SCALE * jax.nn.sigmoid(vec(V_W0) + lora[:, 0:D_RWKV])
    a = jax.nn.sigmoid(vec(V_A0) + lora[:, D_RWKV:2 * D_RWKV])
    gate = lora[:, 2 * D_RWKV:3 * D_RWKV]

    kk = k * vec(V_KK)
    kk = kk * lax.rsqrt(jnp.maximum(_headsum(kk * kk, e_ref), 1e-12))
    k2 = k * (1.0 + (a - 1.0) * vec(V_KA))
    bonus = _headsum(r * k2 * vec(V_RK), e_ref) * v

    r_s[...] = r
    k_s[...] = k2
    v_s[...] = v
    z_s[...] = -kk
    b_s[...] = kk * a
    lw_s[...] = lw

    row_t = lax.broadcasted_iota(jnp.int32, (CHUNK, GROUP), 0)
    col_s = lax.broadcasted_iota(jnp.int32, (CHUNK, GROUP), 1) % CHUNK
    strict = col_s < row_t
    incl = col_s <= row_t
    eye = (col_s == row_t).astype(F32)
    brow = lax.broadcasted_iota(jnp.int32, (GROUP, GROUP), 0) // HEAD
    bcol = lax.broadcasted_iota(jnp.int32, (GROUP, GROUP), 1) // HEAD
    same_head = brow == bcol
    lt_r = lax.broadcasted_iota(jnp.int32, (CHUNK, CHUNK), 0)
    lt_c = lax.broadcasted_iota(jnp.int32, (CHUNK, CHUNK), 1)
    lower = (lt_c <= lt_r).astype(BF16)

    def bd(t):
        tiled = jnp.concatenate([t] * HEADS_PER_GROUP, axis=0)
        return jnp.where(same_head, tiled, 0.0).astype(BF16)

    def per_batch(bi, carry):
        rows = pl.ds(pl.multiple_of(bi * CHUNK, CHUNK), CHUNK)
        lw_b = lw_s[rows, :]
        hi, mid3, lo3 = _split3(lw_b)
        cl_b = _dot(lower, hi) + _dot(lower, mid3) + _dot(lower, lo3)
        for g in range(N_GROUPS):
            lanes = slice(g * GROUP, (g + 1) * GROUP)
            cl = cl_b[:, lanes]
            lw_g = lw_b[:, lanes]
            ref = cl[MID:MID + 1, :]
            e_out = jnp.exp(ref - cl)
            rt = (r_s[rows, lanes] * jnp.exp(cl - ref)).astype(BF16)
            zt = (z_s[rows, lanes] * jnp.exp(cl - lw_g - ref)).astype(BF16)
            kt = k_s[rows, lanes] * e_out
            bt = b_s[rows, lanes] * e_out
            vg = v_s[rows, lanes]
            s0 = (state_ref[bi, g] * jnp.exp(ref)).astype(F32)
            s0b = s0.astype(BF16)

            zr = jnp.concatenate([zt, rt], axis=0)
            a_b = _dot_nt(zr, bd(bt))
            a_k = _dot_nt(zr, bd(kt))
            a_bz = jnp.where(strict, a_b[:CHUNK], 0.0)
            a_br = jnp.where(incl, a_b[CHUNK:], 0.0).astype(BF16)
            a_kz = jnp.where(strict, a_k[:CHUNK], 0.0).astype(BF16)
            a_kr = jnp.where(incl, a_k[CHUNK:], 0.0).astype(BF16)

            t_acc = eye + a_bz
            pw = _dot(a_bz.astype(BF16), bd(a_bz))
            n = 2
            while n < CHUNK // 2:
                res = _dot(jnp.concatenate([t_acc, pw], axis=0).astype(BF16), bd(pw))
                t_acc = t_acc + res[:CHUNK]
                pw = res[CHUNK:]
                n *= 2
            t_acc = t_acc + _dot(t_acc.astype(BF16), bd(pw))

            vbd = bd(vg)
            rhs0 = _dot_nt(zt, s0b) + _dot(a_kz, vbd)
            u = _dot(t_acc.astype(BF16), bd(rhs0))
            y = _dot_nt(rt, s0b) + _dot(a_br, bd(u)) + _dot(a_kr, vbd)
            y_s[rows, lanes] = y

            upd = _dot_tn(jnp.concatenate([u, vg], axis=0).astype(BF16),
                          jnp.concatenate([bt, kt], axis=0).astype(BF16))
            s_new = (s0 + jnp.where(same_head, upd, 0.0)) * jnp.exp(cl[CHUNK - 1:CHUNK, :] - ref)
            state_ref[bi, g] = s_new
        return carry

    lax.fori_loop(0, n_batch, per_batch, 0)

    y = y_s[...]
    mean = _headsum(y, e_ref) * (1.0 / HEAD)
    yc = y - mean
    var = _headsum(yc * yc, e_ref) * (1.0 / HEAD)
    yn = yc * lax.rsqrt(var + GN_EPS) * vec(V_GNW) + vec(V_GNB)
    out_a = (yn + bonus) * gate

    pc = p[:, D_SHIFT:]
    glu = pc[:, :D_CONV] * jax.nn.sigmoid(pc[:, D_CONV:])
    gh_ref[:, CONV_HIST:CONV_HIST + CHUNK, :] = glu.reshape(n_batch, CHUNK, D_CONV)
    acc = jnp.zeros((n_batch, CHUNK, D_CONV), F32) + vec(V_CB)
    base = CONV_HIST - (CONV_WIDTH - 1)
    for w in range(CONV_WIDTH):
        acc = acc + gh_ref[:, base + w:base + w + CHUNK, :] * dw_ref[w:w + 1, :]
    gh_ref[:, 0:CONV_HIST, :] = gh_ref[:, CHUNK:CHUNK + CONV_HIST, :]
    c = acc.reshape(m_rows, D_CONV)
    c_mean = jnp.mean(c, axis=-1, keepdims=True)
    cc = c - c_mean
    c_var = jnp.mean(cc * cc, axis=-1, keepdims=True)
    out_b = jax.nn.silu(cc * lax.rsqrt(c_var + LN_EPS) * vec(V_LNW) + vec(V_LNB))

    m = (_dot(out_a.astype(BF16), wout_ref[0:D_RWKV, :])
         + _dot(out_b.astype(BF16), wout_ref[D_RWKV:, :]))
    o_ref[...] = (x + _rmsnorm(m, gpost_ref[...])).reshape(n_batch, CHUNK, D_MODEL)


def _mixer(x, g_pre, g_post, w_in, mu, w_lora, vec, dw, e_mat, w_out):
    n_batch, seq, _ = x.shape
    m_rows = n_batch * CHUNK
    const = lambda i: (0, 0)
    one = dict(pipeline_mode=pl.Buffered(1))
    tok = lambda: pltpu.VMEM((m_rows, D_RWKV), F32)
    return pl.pallas_call(
        functools.partial(_mixer_kernel, n_batch=n_batch),
        out_shape=jax.ShapeDtypeStruct(x.shape, F32),
        grid=(seq // CHUNK,),
        in_specs=[
            pl.BlockSpec((n_batch, CHUNK, D_MODEL), lambda i: (0, i, 0)),
            pl.BlockSpec((1, D_MODEL), const),
            pl.BlockSpec((1, D_MODEL), const),
            pl.BlockSpec((D_MODEL, D_IN), const, **one),
            pl.BlockSpec((1, D_SHIFT), const),
            pl.BlockSpec((D_LORA, 3 * D_RWKV), const, **one),
            pl.BlockSpec((N_VEC, D_RWKV), const),
            pl.BlockSpec((CONV_HIST, D_CONV), const),
            pl.BlockSpec((D_RWKV, D_RWKV), const, **one),
            pl.BlockSpec((D_MODEL, D_MODEL), const, **one),
        ],
        out_specs=pl.BlockSpec((n_batch, CHUNK, D_MODEL), lambda i: (0, i, 0)),
        scratch_shapes=[
            pltpu.VMEM((n_batch, 1, D_SHIFT), F32),
            pltpu.VMEM((n_batch, CONV_HIST + CHUNK, D_CONV), F32),
            pltpu.VMEM((n_batch, N_GROUPS, GROUP, GROUP), F32),
            tok(), tok(), tok(), tok(), tok(), tok(), tok(),
        ],
        compiler_params=pltpu.CompilerParams(
            dimension_semantics=("arbitrary",), vmem_limit_bytes=VMEM_LIMIT),
        name="mixer",
    )(x, g_pre, g_post, w_in, mu, w_lora, vec, dw, e_mat, w_out)


def kernel(x, ffn1_norm_pre, ffn1_norm_post, ffn1_w_gu, ffn1_w_down, mix_norm_pre, mix_norm_post, w_in, shift_mu, w_up, w0, a_up, a0, g_up, k_k, k_a, r_k, gn_w, gn_b, conv_dw, conv_b, conv_ln_w, conv_ln_b, w_out, ffn2_norm_pre, ffn2_norm_post, ffn2_w_gu, ffn2_w_down):
    n_batch, seq, d_model = x.shape
    depth = ffn1_w_gu.shape[0]
    assert d_model == D_MODEL and seq % CHUNK == 0 and (n_batch * seq) % FFN_TM == 0

    head_id = jnp.arange(D_RWKV) // HEAD
    e_mat = (head_id[:, None] == head_id[None, :]).astype(BF16)

    for l in range(depth):
        w_lora = jnp.zeros((D_LORA, 3 * D_RWKV), F32)
        w_lora = w_lora.at[0:LORA_W, 0:D_RWKV].set(w_up[l])
        w_lora = w_lora.at[LORA_W:LORA_W + LORA_A, D_RWKV:2 * D_RWKV].set(a_up[l])
        w_lora = w_lora.at[LORA_W + LORA_A:, 2 * D_RWKV:].set(g_up[l])
        rows = [w0[l], a0[l], k_k[l], k_a[l], r_k[l].reshape(D_RWKV), gn_w[l], gn_b[l],
                conv_b[l], conv_ln_w[l], conv_ln_b[l]]
        vec = jnp.zeros((N_VEC, D_RWKV), F32).at[0:len(rows)].set(jnp.stack(rows))
        dw = jnp.zeros((CONV_HIST, D_CONV), F32).at[0:CONV_WIDTH].set(conv_dw[l])

        x2d = _ffn(x.reshape(n_batch * seq, D_MODEL),
                   ffn1_norm_pre[l].reshape(1, D_MODEL), ffn1_norm_post[l].reshape(1, D_MODEL),
                   ffn1_w_gu[l].astype(BF16), ffn1_w_down[l].astype(BF16))
        x = _mixer(x2d.reshape(n_batch, seq, D_MODEL),
                   mix_norm_pre[l].reshape(1, D_MODEL), mix_norm_post[l].reshape(1, D_MODEL),
                   w_in[l].astype(BF16), shift_mu[l].reshape(1, D_SHIFT), w_lora.astype(BF16),
                   vec, dw, e_mat, w_out[l].astype(BF16))
        x2d = _ffn(x.reshape(n_batch * seq, D_MODEL),
                   ffn2_norm_pre[l].reshape(1, D_MODEL), ffn2_norm_post[l].reshape(1, D_MODEL),
                   ffn2_w_gu[l].astype(BF16), ffn2_w_down[l].astype(BF16))
        x = x2d.reshape(n_batch, seq, D_MODEL)
    return x
```

```python
import functools
import math

import jax
import jax.numpy as jnp
from jax import lax
from jax.experimental import pallas as pl
from jax.experimental.pallas import tpu as pltpu

F32 = jnp.float32
BF16 = jnp.bfloat16

D_MODEL = 1024
D_FF = 2816
D_RWKV = 512
D_CONV = 512
HEAD = 64
LORA_W = 64
LORA_A = 64
LORA_G = 128
D_LORA = LORA_W + LORA_A + LORA_G
D_SHIFT = 3 * D_RWKV + D_LORA
D_IN = D_SHIFT + 2 * D_CONV
CONV_WIDTH = 31
RMS_EPS = 1e-6
GN_EPS = 64e-5
LN_EPS = 1e-5
DECAY_SCALE = math.exp(-0.5)

CHUNK = 64
GROUP = 256
N_GROUPS = D_RWKV // GROUP
HEADS_PER_GROUP = GROUP // HEAD
MID = CHUNK // 2 - 1
CONV_HIST = 32

FFN_TM = 512
FFN_FC = 256
VMEM_LIMIT = 56 * 1024 * 1024

(V_W0, V_A0, V_KK, V_KA, V_RK, V_GNW, V_GNB, V_CB, V_LNW, V_LNB) = range(10)
N_VEC = 16


def _rmsnorm(x, g):
    ms = jnp.mean(x * x, axis=-1, keepdims=True)
    return x * lax.rsqrt(ms + RMS_EPS) * g


def _dot(a, b):
    return jnp.dot(a, b, preferred_element_type=F32)


def _dot_nt(a, b):
    return lax.dot_general(a, b, (((1,), (1,)), ((), ())), preferred_element_type=F32)


def _dot_tn(a, b):
    return lax.dot_general(a, b, (((0,), (0,)), ((), ())), preferred_element_type=F32)


def _ffn_kernel(x_ref, gpre_ref, gpost_ref, wgu_ref, wd_ref, o_ref, act_ref):
    x = x_ref[...]
    h = _rmsnorm(x, gpre_ref[...]).astype(BF16)
    for c in range(D_FF // FFN_FC):
        lo = c * FFN_FC
        gate = _dot(h, wgu_ref[:, lo:lo + FFN_FC])
        up = _dot(h, wgu_ref[:, D_FF + lo:D_FF + lo + FFN_FC])
        act_ref[:, lo:lo + FFN_FC] = (jax.nn.silu(gate) * up).astype(BF16)
    f = _dot(act_ref[...], wd_ref[...])
    o_ref[...] = x + 0.5 * _rmsnorm(f, gpost_ref[...])


def _ffn(x2d, g_pre, g_post, w_gu, w_down):
    n_tok = x2d.shape[0]
    const = lambda i: (0, 0)
    return pl.pallas_call(
        _ffn_kernel,
        out_shape=jax.ShapeDtypeStruct((n_tok, D_MODEL), F32),
        grid=(n_tok // FFN_TM,),
        in_specs=[
            pl.BlockSpec((FFN_TM, D_MODEL), lambda i: (i, 0)),
            pl.BlockSpec((1, D_MODEL), const),
            pl.BlockSpec((1, D_MODEL), const),
            pl.BlockSpec((D_MODEL, 2 * D_FF), const, pipeline_mode=pl.Buffered(1)),
            pl.BlockSpec((D_FF, D_MODEL), const, pipeline_mode=pl.Buffered(1)),
        ],
        out_specs=pl.BlockSpec((FFN_TM, D_MODEL), lambda i: (i, 0)),
        scratch_shapes=[pltpu.VMEM((FFN_TM, D_FF), BF16)],
        compiler_params=pltpu.CompilerParams(
            dimension_semantics=("arbitrary",), vmem_limit_bytes=VMEM_LIMIT),
        name="ffn",
    )(x2d, g_pre, g_post, w_gu, w_down)


def _split3(x):
    hi = x.astype(BF16)
    r1 = x - hi.astype(F32)
    mid = r1.astype(BF16)
    lo = (r1 - mid.astype(F32)).astype(BF16)
    return hi, mid, lo


def _headsum(x, e_ref):
    hi = x.astype(BF16)
    lo = (x - hi.astype(F32)).astype(BF16)
    e = e_ref[...]
    return _dot(hi, e) + _dot(lo, e)


def _mixer_kernel(x_ref, gpre_ref, gpost_ref, win_ref, mu_ref, wlora_ref, vec_ref,
                  dw_ref, e_ref, wout_ref, o_ref,
                  carry_ref, gh_ref, state_ref,
                  r_s, k_s, v_s, z_s, b_s, lw_s, y_s, *, n_batch):
    step = pl.program_id(0)
    m_rows = n_batch * CHUNK

    @pl.when(step == 0)
    def _init():
        carry_ref[...] = jnp.zeros_like(carry_ref)
        gh_ref[:, 0:CONV_HIST, :] = jnp.zeros((n_batch, CONV_HIST, D_CONV), F32)
        state_ref[...] = jnp.zeros_like(state_ref)

    vec = lambda row: vec_ref[row:row + 1, :]

    x = x_ref[...].reshape(m_rows, D_MODEL)
    h = _rmsnorm(x, gpre_ref[...]).astype(BF16)
    p = _dot(h, win_ref[...])

    ps = p[:, :D_SHIFT].reshape(n_batch, CHUNK, D_SHIFT)
    prev = pltpu.roll(ps, 1, axis=1)
    frame = lax.broadcasted_iota(jnp.int32, (1, CHUNK, 1), 1)
    prev = jnp.where(frame == 0, carry_ref[...], prev)
    carry_ref[...] = ps[:, CHUNK - 1:CHUNK, :]
    ps = (ps + (prev - ps) * mu_ref[...]).reshape(m_rows, D_SHIFT)

    r = ps[:, 0:D_RWKV]
    k = ps[:, D_RWKV:2 * D_RWKV]
    v = ps[:, 2 * D_RWKV:3 * D_RWKV]
    xl = ps[:, 3 * D_RWKV:D_SHIFT]

    lane = lax.broadcasted_iota(jnp.int32, (1, D_LORA), 1)
    lact = jnp.where(lane < LORA_W, jnp.tanh(xl),
                     jnp.where(lane < LORA_W + LORA_A, xl, jax.nn.sigmoid(xl)))
    lora = _dot(lact.astype(BF16), wlora_ref[...])
    lw = -DECAY_SCALE * jax.nn.sigmoid(vec(V_W0) + lora[:, 0:D_RWKV])
    a = jax.nn.sigmoid(vec(V_A0) + lora[:, D_RWKV:2 * D_RWKV])
    gate = lora[:, 2 * D_RWKV:3 * D_RWKV]

    kk = k * vec(V_KK)
    kk = kk * lax.rsqrt(jnp.maximum(_headsum(kk * kk, e_ref), 1e-12))
    k2 = k * (1.0 + (a - 1.0) * vec(V_KA))
    bonus = _headsum(r * k2 * vec(V_RK), e_ref) * v

    r_s[...] = r
    k_s[...] = k2
    v_s[...] = v
    z_s[...] = -kk
    b_s[...] = kk * a
    lw_s[...] = lw

    row_t = lax.broadcasted_iota(jnp.int32, (CHUNK, GROUP), 0)
    col_s = lax.broadcasted_iota(jnp.int32, (CHUNK, GROUP), 1) % CHUNK
    strict = col_s < row_t
    incl = col_s <= row_t
    eye = (col_s == row_t).astype(F32)
    brow = lax.broadcasted_iota(jnp.int32, (GROUP, GROUP), 0) // HEAD
    bcol = lax.broadcasted_iota(jnp.int32, (GROUP, GROUP), 1) // HEAD
    same_head = brow == bcol
    lt_r = lax.broadcasted_iota(jnp.int32, (CHUNK, CHUNK), 0)
    lt_c = lax.broadcasted_iota(jnp.int32, (CHUNK, CHUNK), 1)
    lower = (lt_c <= lt_r).astype(BF16)

    def bd(t):
        tiled = jnp.concatenate([t] * HEADS_PER_GROUP, axis=0)
        return jnp.where(same_head, tiled, 0.0).astype(BF16)

    chains = [(bi, g) for bi in range(n_batch) for g in range(N_GROUPS)]
    stage = lambda fn, *lists: [fn(*args) for args in zip(*lists)]

    def cumulate(bi):
        lw_b = lw_s[bi * CHUNK:(bi + 1) * CHUNK, :]
        hi, mid3, lo3 = _split3(lw_b)
        return lw_b, _dot(lower, hi) + _dot(lower, mid3) + _dot(lower, lo3)
    cums = [cumulate(bi) for bi in range(n_batch)]

    def prepare(chain):
        bi, g = chain
        rows = slice(bi * CHUNK, (bi + 1) * CHUNK)
        lanes = slice(g * GROUP, (g + 1) * GROUP)
        lw_g = cums[bi][0][:, lanes]
        cl = cums[bi][1][:, lanes]
        ref = cl[MID:MID + 1, :]
        e_out = jnp.exp(ref - cl)
        rt = (r_s[rows, lanes] * jnp.exp(cl - ref)).astype(BF16)
        zt = (z_s[rows, lanes] * jnp.exp(cl - lw_g - ref)).astype(BF16)
        kt = k_s[rows, lanes] * e_out
        bt = b_s[rows, lanes] * e_out
        vg = v_s[rows, lanes]
        s0 = state_ref[bi, g] * jnp.exp(ref)
        decay_end = jnp.exp(cl[CHUNK - 1:CHUNK, :] - ref)
        return dict(rt=rt, zt=zt, kt=kt, bt=bt, vg=vg, s0=s0, s0b=s0.astype(BF16),
                    decay_end=decay_end)
    c = stage(prepare, chains)

    def a_matrices(ci):
        zr = jnp.concatenate([ci["zt"], ci["rt"]], axis=0)
        a_b = _dot_nt(zr, bd(ci["bt"]))
        a_k = _dot_nt(zr, bd(ci["kt"]))
        a_bz = jnp.where(strict, a_b[:CHUNK], 0.0)
        return dict(ci, a_bz=a_bz,
                    a_br=jnp.where(incl, a_b[CHUNK:], 0.0).astype(BF16),
                    a_kz=jnp.where(strict, a_k[:CHUNK], 0.0).astype(BF16),
                    a_kr=jnp.where(incl, a_k[CHUNK:], 0.0).astype(BF16))
    c = stage(a_matrices, c)

    t_acc = stage(lambda ci: eye + ci["a_bz"], c)
    pw = stage(lambda ci: _dot(ci["a_bz"].astype(BF16), bd(ci["a_bz"])), c)
    n = 2
    while n < CHUNK // 2:
        res = stage(lambda t, q: _dot(jnp.concatenate([t, q], axis=0).astype(BF16), bd(q)), t_acc, pw)
        t_acc = stage(lambda t, rr: t + rr[:CHUNK], t_acc, res)
        pw = stage(lambda rr: rr[CHUNK:], res)
        n *= 2
    t_acc = stage(lambda t, q: t + _dot(t.astype(BF16), bd(q)), t_acc, pw)

    vbd = stage(lambda ci: bd(ci["vg"]), c)
    rhs0 = stage(lambda ci, vb: _dot_nt(ci["zt"], ci["s0b"]) + _dot(ci["a_kz"], vb), c, vbd)
    u = stage(lambda t, q: _dot(t.astype(BF16), bd(q)), t_acc, rhs0)
    y = stage(lambda ci, ui, vb: _dot_nt(ci["rt"], ci["s0b"]) + _dot(ci["a_br"], bd(ui))
              + _dot(ci["a_kr"], vb), c, u, vbd)
    upd = stage(lambda ci, ui: _dot_tn(jnp.concatenate([ui, ci["vg"]], axis=0).astype(BF16),
                                       jnp.concatenate([ci["bt"], ci["kt"]], axis=0).astype(BF16)), c, u)
    for (bi, g), ci, yi, di in zip(chains, c, y, upd):
        y_s[bi * CHUNK:(bi + 1) * CHUNK, g * GROUP:(g + 1) * GROUP] = yi
        state_ref[bi, g] = (ci["s0"] + jnp.where(same_head, di, 0.0)) * ci["decay_end"]

    y = y_s[...]
    mean = _headsum(y, e_ref) * (1.0 / HEAD)
    yc = y - mean
    var = _headsum(yc * yc, e_ref) * (1.0 / HEAD)
    yn = yc * lax.rsqrt(var + GN_EPS) * vec(V_GNW) + vec(V_GNB)
    out_a = (yn + bonus) * gate

    pc = p[:, D_SHIFT:]
    glu = pc[:, :D_CONV] * jax.nn.sigmoid(pc[:, D_CONV:])
    gh_ref[:, CONV_HIST:CONV_HIST + CHUNK, :] = glu.reshape(n_batch, CHUNK, D_CONV)
    acc = jnp.zeros((n_batch, CHUNK, D_CONV), F32) + vec(V_CB)
    base = CONV_HIST - (CONV_WIDTH - 1)
    for w in range(CONV_WIDTH):
        acc = acc + gh_ref[:, base + w:base + w + CHUNK, :] * dw_ref[w:w + 1, :]
    gh_ref[:, 0:CONV_HIST, :] = gh_ref[:, CHUNK:CHUNK + CONV_HIST, :]
    c = acc.reshape(m_rows, D_CONV)
    c_mean = jnp.mean(c, axis=-1, keepdims=True)
    cc = c - c_mean
    c_var = jnp.mean(cc * cc, axis=-1, keepdims=True)
    out_b = jax.nn.silu(cc * lax.rsqrt(c_var + LN_EPS) * vec(V_LNW) + vec(V_LNB))

    m = (_dot(out_a.astype(BF16), wout_ref[0:D_RWKV, :])
         + _dot(out_b.astype(BF16), wout_ref[D_RWKV:, :]))
    o_ref[...] = (x + _rmsnorm(m, gpost_ref[...])).reshape(n_batch, CHUNK, D_MODEL)


def _mixer(x, g_pre, g_post, w_in, mu, w_lora, vec, dw, e_mat, w_out):
    n_batch, seq, _ = x.shape
    m_rows = n_batch * CHUNK
    const = lambda i: (0, 0)
    one = dict(pipeline_mode=pl.Buffered(1))
    tok = lambda: pltpu.VMEM((m_rows, D_RWKV), F32)
    return pl.pallas_call(
        functools.partial(_mixer_kernel, n_batch=n_batch),
        out_shape=jax.ShapeDtypeStruct(x.shape, F32),
        grid=(seq // CHUNK,),
        in_specs=[
            pl.BlockSpec((n_batch, CHUNK, D_MODEL), lambda i: (0, i, 0)),
            pl.BlockSpec((1, D_MODEL), const),
            pl.BlockSpec((1, D_MODEL), const),
            pl.BlockSpec((D_MODEL, D_IN), const, **one),
            pl.BlockSpec((1, D_SHIFT), const),
            pl.BlockSpec((D_LORA, 3 * D_RWKV), const, **one),
            pl.BlockSpec((N_VEC, D_RWKV), const),
            pl.BlockSpec((CONV_HIST, D_CONV), const),
            pl.BlockSpec((D_RWKV, D_RWKV), const, **one),
            pl.BlockSpec((D_MODEL, D_MODEL), const, **one),
        ],
        out_specs=pl.BlockSpec((n_batch, CHUNK, D_MODEL), lambda i: (0, i, 0)),
        scratch_shapes=[
            pltpu.VMEM((n_batch, 1, D_SHIFT), F32),
            pltpu.VMEM((n_batch, CONV_HIST + CHUNK, D_CONV), F32),
            pltpu.VMEM((n_batch, N_GROUPS, GROUP, GROUP), F32),
            tok(), tok(), tok(), tok(), tok(), tok(), tok(),
        ],
        compiler_params=pltpu.CompilerParams(
            dimension_semantics=("arbitrary",), vmem_limit_bytes=VMEM_LIMIT),
        name="mixer",
    )(x, g_pre, g_post, w_in, mu, w_lora, vec, dw, e_mat, w_out)


def kernel(x, ffn1_norm_pre, ffn1_norm_post, ffn1_w_gu, ffn1_w_down, mix_norm_pre, mix_norm_post, w_in, shift_mu, w_up, w0, a_up, a0, g_up, k_k, k_a, r_k, gn_w, gn_b, conv_dw, conv_b, conv_ln_w, conv_ln_b, w_out, ffn2_norm_pre, ffn2_norm_post, ffn2_w_gu, ffn2_w_down):
    n_batch, seq, d_model = x.shape
    depth = ffn1_w_gu.shape[0]
    assert d_model == D_MODEL and seq % CHUNK == 0 and (n_batch * seq) % FFN_TM == 0

    head_id = jnp.arange(D_RWKV) // HEAD
    e_mat = (head_id[:, None] == head_id[None, :]).astype(BF16)

    for l in range(depth):
        w_lora = jnp.zeros((D_LORA, 3 * D_RWKV), F32)
        w_lora = w_lora.at[0:LORA_W, 0:D_RWKV].set(w_up[l])
        w_lora = w_lora.at[LORA_W:LORA_W + LORA_A, D_RWKV:2 * D_RWKV].set(a_up[l])
        w_lora = w_lora.at[LORA_W + LORA_A:, 2 * D_RWKV:].set(g_up[l])
        rows = [w0[l], a0[l], k_k[l], k_a[l], r_k[l].reshape(D_RWKV), gn_w[l], gn_b[l],
                conv_b[l], conv_ln_w[l], conv_ln_b[l]]
        vec = jnp.zeros((N_VEC, D_RWKV), F32).at[0:len(rows)].set(jnp.stack(rows))
        dw = jnp.zeros((CONV_HIST, D_CONV), F32).at[0:CONV_WIDTH].set(conv_dw[l])

        x2d = _ffn(x.reshape(n_batch * seq, D_MODEL),
                   ffn1_norm_pre[l].reshape(1, D_MODEL), ffn1_norm_post[l].reshape(1, D_MODEL),
                   ffn1_w_gu[l].astype(BF16), ffn1_w_down[l].astype(BF16))
        x = _mixer(x2d.reshape(n_batch, seq, D_MODEL),
                   mix_norm_pre[l].reshape(1, D_MODEL), mix_norm_post[l].reshape(1, D_MODEL),
                   w_in[l].astype(BF16), shift_mu[l].reshape(1, D_SHIFT), w_lora.astype(BF16),
                   vec, dw, e_mat, w_out[l].astype(BF16))
        x2d = _ffn(x.reshape(n_batch * seq, D_MODEL),
                   ffn2_norm_pre[l].reshape(1, D_MODEL), ffn2_norm_post[l].reshape(1, D_MODEL),
                   ffn2_w_gu[l].astype(BF16), ffn2_w_down[l].astype(BF16))
        x = x2d.reshape(n_batch, seq, D_MODEL)
    return x
```

```python
import functools
import math

import jax
import jax.numpy as jnp
from jax import lax
from jax.experimental import pallas as pl
from jax.experimental.pallas import tpu as pltpu

F32 = jnp.float32
BF16 = jnp.bfloat16

D_MODEL = 1024
D_FF = 2816
D_RWKV = 512
D_CONV = 512
HEAD = 64
LORA_W = 64
LORA_A = 64
LORA_G = 128
D_LORA = LORA_W + LORA_A + LORA_G
D_SHIFT = 3 * D_RWKV + D_LORA
D_IN = D_SHIFT + 2 * D_CONV
CONV_WIDTH = 31
RMS_EPS = 1e-6
GN_EPS = 64e-5
LN_EPS = 1e-5
DECAY_SCALE = math.exp(-0.5)

CHUNK = 64
GROUP = 256
N_GROUPS = D_RWKV // GROUP
HEADS_PER_GROUP = GROUP // HEAD
MID = CHUNK // 2 - 1
CONV_HIST = 32
LANE_TILE = 128
assert CHUNK == HEAD

FFN_TM = 512
FFN_FC = 256
VMEM_LIMIT = 56 * 1024 * 1024

(V_W0, V_A0, V_KK, V_KA, V_RK, V_GNW, V_GNB, V_CB, V_LNW, V_LNB) = range(10)
N_VEC = 16


def _rmsnorm(x, g):
    ms = jnp.mean(x * x, axis=-1, keepdims=True)
    return x * lax.rsqrt(ms + RMS_EPS) * g


def _dot(a, b):
    return jnp.dot(a, b, preferred_element_type=F32)


def _dot_nt(a, b):
    return lax.dot_general(a, b, (((1,), (1,)), ((), ())), preferred_element_type=F32)


def _dot_tn(a, b):
    return lax.dot_general(a, b, (((0,), (0,)), ((), ())), preferred_element_type=F32)


def _ffn_kernel(x_ref, gpre_ref, gpost_ref, wgu_ref, wd_ref, o_ref, act_ref):
    x = x_ref[...]
    h = _rmsnorm(x, gpre_ref[...]).astype(BF16)
    for c in range(D_FF // FFN_FC):
        lo = c * FFN_FC
        gate = _dot(h, wgu_ref[:, lo:lo + FFN_FC])
        up = _dot(h, wgu_ref[:, D_FF + lo:D_FF + lo + FFN_FC])
        act_ref[:, lo:lo + FFN_FC] = (jax.nn.silu(gate) * up).astype(BF16)
    f = _dot(act_ref[...], wd_ref[...])
    o_ref[...] = x + 0.5 * _rmsnorm(f, gpost_ref[...])


def _ffn(x2d, g_pre, g_post, w_gu, w_down):
    n_tok = x2d.shape[0]
    const = lambda i: (0, 0)
    return pl.pallas_call(
        _ffn_kernel,
        out_shape=jax.ShapeDtypeStruct((n_tok, D_MODEL), F32),
        grid=(n_tok // FFN_TM,),
        in_specs=[
            pl.BlockSpec((FFN_TM, D_MODEL), lambda i: (i, 0)),
            pl.BlockSpec((1, D_MODEL), const),
            pl.BlockSpec((1, D_MODEL), const),
            pl.BlockSpec((D_MODEL, 2 * D_FF), const, pipeline_mode=pl.Buffered(1)),
            pl.BlockSpec((D_FF, D_MODEL), const, pipeline_mode=pl.Buffered(1)),
        ],
        out_specs=pl.BlockSpec((FFN_TM, D_MODEL), lambda i: (i, 0)),
        scratch_shapes=[pltpu.VMEM((FFN_TM, D_FF), BF16)],
        compiler_params=pltpu.CompilerParams(
            dimension_semantics=("arbitrary",), vmem_limit_bytes=VMEM_LIMIT),
        name="ffn",
    )(x2d, g_pre, g_post, w_gu, w_down)


def _split3(x):
    hi = x.astype(BF16)
    r1 = x - hi.astype(F32)
    mid = r1.astype(BF16)
    lo = (r1 - mid.astype(F32)).astype(BF16)
    return hi, mid, lo


def _headsum(x, e_ref, split=False):
    e = e_ref[...]
    hi = x.astype(BF16)
    lo = (x - hi.astype(F32)).astype(BF16) if split else None
    out = []
    for g in range(N_GROUPS):
        lanes = slice(g * GROUP, (g + 1) * GROUP)
        s = _dot(hi[:, lanes], e)
        if split:
            s = s + _dot(lo[:, lanes], e)
        out.append(s)
    return jnp.concatenate(out, axis=-1)


def _mixer_kernel(x_ref, gpre_ref, gpost_ref, win_ref, mu_ref, wlora_ref, vec_ref,
                  dw_ref, e_ref, wout_ref, o_ref,
                  carry_ref, gh_ref, state_ref,
                  r_s, k_s, v_s, z_s, b_s, lw_s, y_s, *, n_batch):
    step = pl.program_id(0)
    m_rows = n_batch * CHUNK

    @pl.when(step == 0)
    def _init():
        carry_ref[...] = jnp.zeros_like(carry_ref)
        gh_ref[:, 0:CONV_HIST, :] = jnp.zeros((n_batch, CONV_HIST, D_CONV), F32)
        state_ref[...] = jnp.zeros_like(state_ref)

    vec = lambda row: vec_ref[row:row + 1, :]

    x = x_ref[...].reshape(m_rows, D_MODEL)
    h = _rmsnorm(x, gpre_ref[...]).astype(BF16)
    p = _dot(h, win_ref[...])

    ps = p[:, :D_SHIFT].reshape(n_batch, CHUNK, D_SHIFT)
    prev = pltpu.roll(ps, 1, axis=1)
    frame = lax.broadcasted_iota(jnp.int32, (1, CHUNK, 1), 1)
    prev = jnp.where(frame == 0, carry_ref[...], prev)
    carry_ref[...] = ps[:, CHUNK - 1:CHUNK, :]
    ps = (ps + (prev - ps) * mu_ref[...]).reshape(m_rows, D_SHIFT)

    r = ps[:, 0:D_RWKV]
    k = ps[:, D_RWKV:2 * D_RWKV]
    v = ps[:, 2 * D_RWKV:3 * D_RWKV]
    xl = ps[:, 3 * D_RWKV:D_SHIFT]

    lane = lax.broadcasted_iota(jnp.int32, (1, D_LORA), 1)
    lact = jnp.where(lane < LORA_W, jnp.tanh(xl),
                     jnp.where(lane < LORA_W + LORA_A, xl, jax.nn.sigmoid(xl)))
    lora = _dot(lact.astype(BF16), wlora_ref[...])
    lw = -DECAY_SCALE * jax.nn.sigmoid(vec(V_W0) + lora[:, 0:D_RWKV])
    a = jax.nn.sigmoid(vec(V_A0) + lora[:, D_RWKV:2 * D_RWKV])
    gate = lora[:, 2 * D_RWKV:3 * D_RWKV]

    kk = k * vec(V_KK)
    kk = kk * lax.rsqrt(jnp.maximum(_headsum(kk * kk, e_ref), 1e-12))
    k2 = k * (1.0 + (a - 1.0) * vec(V_KA))
    bonus = _headsum(r * k2 * vec(V_RK), e_ref) * v

    r_s[...] = r
    k_s[...] = k2
    v_s[...] = v
    z_s[...] = -kk
    b_s[...] = kk * a
    lw_s[...] = lw

    row_t = lax.broadcasted_iota(jnp.int32, (CHUNK, GROUP), 0)
    col_s = lax.broadcasted_iota(jnp.int32, (CHUNK, GROUP), 1) % CHUNK
    strict = col_s < row_t
    incl = col_s <= row_t
    eye = (col_s == row_t).astype(F32)
    lt_r = lax.broadcasted_iota(jnp.int32, (CHUNK, CHUNK), 0)
    lt_c = lax.broadcasted_iota(jnp.int32, (CHUNK, CHUNK), 1)
    lower = (lt_c <= lt_r).astype(BF16)

    half_id = lax.broadcasted_iota(jnp.int32, (CHUNK, LANE_TILE), 1) // HEAD
    in_half = [half_id == 0, half_id == 1]
    half_keep = [m.astype(BF16) for m in in_half]
    tile_of = lambda hd: slice(LANE_TILE * (hd // 2), LANE_TILE * (hd // 2 + 1))
    zero_tile = jnp.zeros((CHUNK, LANE_TILE), BF16)

    def place(tiles):
        return jnp.concatenate(
            [jnp.concatenate([tiles[hd] if j == hd // 2 else zero_tile
                              for j in range(GROUP // LANE_TILE)], axis=1)
             for hd in range(HEADS_PER_GROUP)], axis=0)

    def bd(t):
        tb = t.astype(BF16)
        return place([tb[:, tile_of(hd)] * half_keep[hd % 2] for hd in range(HEADS_PER_GROUP)])

    chains = [(bi, g) for bi in range(n_batch) for g in range(N_GROUPS)]
    stage = lambda fn, *lists: [fn(*args) for args in zip(*lists)]

    def cumulate(bi):
        lw_b = lw_s[bi * CHUNK:(bi + 1) * CHUNK, :]
        hi, mid3, lo3 = _split3(lw_b)
        return lw_b, _dot(lower, hi) + _dot(lower, mid3) + _dot(lower, lo3)
    cums = [cumulate(bi) for bi in range(n_batch)]

    def prepare(chain):
        bi, g = chain
        rows = slice(bi * CHUNK, (bi + 1) * CHUNK)
        lanes = slice(g * GROUP, (g + 1) * GROUP)
        lw_g = cums[bi][0][:, lanes]
        cl = cums[bi][1][:, lanes]
        ref = cl[MID:MID + 1, :]
        e_out = jnp.exp(ref - cl)
        rt = (r_s[rows, lanes] * jnp.exp(cl - ref)).astype(BF16)
        zt = (z_s[rows, lanes] * jnp.exp(cl - lw_g - ref)).astype(BF16)
        kt = k_s[rows, lanes] * e_out
        bt = b_s[rows, lanes] * e_out
        vg = v_s[rows, lanes]
        st = state_ref[bi, g]
        e_ref0 = jnp.exp(ref)
        s0 = [st[HEAD * hd:HEAD * (hd + 1), :] * e_ref0[:, tile_of(hd)]
              for hd in range(HEADS_PER_GROUP)]
        decay_end = jnp.exp(cl[CHUNK - 1:CHUNK, :] - ref)
        return dict(rt=rt, zt=zt, kt=kt, bt=bt, vg=vg, s0=s0,
                    s0b=place([s.astype(BF16) for s in s0]), decay_end=decay_end)
    c = stage(prepare, chains)

    def a_matrices(ci):
        zr = jnp.concatenate([ci["zt"], ci["rt"]], axis=0)
        a_b = _dot_nt(zr, bd(ci["bt"]))
        a_k = _dot_nt(zr, bd(ci["kt"]))
        a_bz = jnp.where(strict, a_b[:CHUNK], 0.0)
        return dict(ci, a_bz=a_bz,
                    a_br=jnp.where(incl, a_b[CHUNK:], 0.0).astype(BF16),
                    a_kz=jnp.where(strict, a_k[:CHUNK], 0.0).astype(BF16),
                    a_kr=jnp.where(incl, a_k[CHUNK:], 0.0).astype(BF16))
    c = stage(a_matrices, c)

    t_acc = stage(lambda ci: eye + ci["a_bz"], c)
    pw = stage(lambda ci: _dot(ci["a_bz"].astype(BF16), bd(ci["a_bz"])), c)
    n = 2
    while n < CHUNK // 2:
        res = stage(lambda t, q: _dot(jnp.concatenate([t, q], axis=0).astype(BF16), bd(q)), t_acc, pw)
        t_acc = stage(lambda t, rr: t + rr[:CHUNK], t_acc, res)
        pw = stage(lambda rr: rr[CHUNK:], res)
        n *= 2
    t_acc = stage(lambda t, q: t + _dot(t.astype(BF16), bd(q)), t_acc, pw)

    vbd = stage(lambda ci: bd(ci["vg"]), c)
    rhs0 = stage(lambda ci, vb: _dot_nt(ci["zt"], ci["s0b"]) + _dot(ci["a_kz"], vb), c, vbd)
    u = stage(lambda t, q: _dot(t.astype(BF16), bd(q)), t_acc, rhs0)
    y = stage(lambda ci, ui, vb: _dot_nt(ci["rt"], ci["s0b"]) + _dot(ci["a_br"], bd(ui))
              + _dot(ci["a_kr"], vb), c, u, vbd)
    upd = stage(lambda ci, ui: _dot_tn(jnp.concatenate([ui, ci["vg"]], axis=0).astype(BF16),
                                       jnp.concatenate([ci["bt"], ci["kt"]], axis=0).astype(BF16)), c, u)
    for (bi, g), ci, yi, di in zip(chains, c, y, upd):
        y_s[bi * CHUNK:(bi + 1) * CHUNK, g * GROUP:(g + 1) * GROUP] = yi
        state_ref[bi, g] = jnp.concatenate(
            [(ci["s0"][hd] + jnp.where(in_half[hd % 2], di[HEAD * hd:HEAD * (hd + 1), tile_of(hd)], 0.0))
             * ci["decay_end"][:, tile_of(hd)] for hd in range(HEADS_PER_GROUP)], axis=0)

    y = y_s[...]
    mean = _headsum(y, e_ref, split=True) * (1.0 / HEAD)
    yc = y - mean
    var = _headsum(yc * yc, e_ref) * (1.0 / HEAD)
    yn = yc * lax.rsqrt(var + GN_EPS) * vec(V_GNW) + vec(V_GNB)
    out_a = (yn + bonus) * gate

    pc = p[:, D_SHIFT:]
    glu = pc[:, :D_CONV] * jax.nn.sigmoid(pc[:, D_CONV:])
    gh_ref[:, CONV_HIST:CONV_HIST + CHUNK, :] = glu.reshape(n_batch, CHUNK, D_CONV)
    conv_rows = []
    for bi in range(n_batch):
        gh = gh_ref[bi]
        acc = jnp.zeros((CHUNK, D_CONV), F32) + vec(V_CB)
        for r in range(8):
            shifted = pltpu.roll(gh, r, axis=0) if r else gh
            for q in range(CONV_HIST // 8):
                s = 8 * q + r
                if s >= CONV_WIDTH:
                    continue
                w = CONV_WIDTH - 1 - s
                lo = CONV_HIST - 8 * q
                acc = acc + shifted[lo:lo + CHUNK, :] * dw_ref[w:w + 1, :]
        conv_rows.append(acc)
    gh_ref[:, 0:CONV_HIST, :] = gh_ref[:, CHUNK:CHUNK + CONV_HIST, :]
    c = jnp.concatenate(conv_rows, axis=0)
    c_mean = jnp.mean(c, axis=-1, keepdims=True)
    cc = c - c_mean
    c_var = jnp.mean(cc * cc, axis=-1, keepdims=True)
    out_b = jax.nn.silu(cc * lax.rsqrt(c_var + LN_EPS) * vec(V_LNW) + vec(V_LNB))

    m = (_dot(out_a.astype(BF16), wout_ref[0:D_RWKV, :])
         + _dot(out_b.astype(BF16), wout_ref[D_RWKV:, :]))
    o_ref[...] = (x + _rmsnorm(m, gpost_ref[...])).reshape(n_batch, CHUNK, D_MODEL)


def _mixer(x, g_pre, g_post, w_in, mu, w_lora, vec, dw, e_mat, w_out):
    n_batch, seq, _ = x.shape
    m_rows = n_batch * CHUNK
    const = lambda i: (0, 0)
    one = dict(pipeline_mode=pl.Buffered(1))
    tok = lambda: pltpu.VMEM((m_rows, D_RWKV), F32)
    return pl.pallas_call(
        functools.partial(_mixer_kernel, n_batch=n_batch),
        out_shape=jax.ShapeDtypeStruct(x.shape, F32),
        grid=(seq // CHUNK,),
        in_specs=[
            pl.BlockSpec((n_batch, CHUNK, D_MODEL), lambda i: (0, i, 0)),
            pl.BlockSpec((1, D_MODEL), const),
            pl.BlockSpec((1, D_MODEL), const),
            pl.BlockSpec((D_MODEL, D_IN), const, **one),
            pl.BlockSpec((1, D_SHIFT), const),
            pl.BlockSpec((D_LORA, 3 * D_RWKV), const, **one),
            pl.BlockSpec((N_VEC, D_RWKV), const),
            pl.BlockSpec((CONV_HIST, D_CONV), const),
            pl.BlockSpec((GROUP, GROUP), const, **one),
            pl.BlockSpec((D_MODEL, D_MODEL), const, **one),
        ],
        out_specs=pl.BlockSpec((n_batch, CHUNK, D_MODEL), lambda i: (0, i, 0)),
        scratch_shapes=[
            pltpu.VMEM((n_batch, 1, D_SHIFT), F32),
            pltpu.VMEM((n_batch, CONV_HIST + CHUNK, D_CONV), F32),
            pltpu.VMEM((n_batch, N_GROUPS, GROUP, LANE_TILE), F32),
            tok(), tok(), tok(), tok(), tok(), tok(), tok(),
        ],
        compiler_params=pltpu.CompilerParams(
            dimension_semantics=("arbitrary",), vmem_limit_bytes=VMEM_LIMIT),
        name="mixer",
    )(x, g_pre, g_post, w_in, mu, w_lora, vec, dw, e_mat, w_out)


def kernel(x, ffn1_norm_pre, ffn1_norm_post, ffn1_w_gu, ffn1_w_down, mix_norm_pre, mix_norm_post, w_in, shift_mu, w_up, w0, a_up, a0, g_up, k_k, k_a, r_k, gn_w, gn_b, conv_dw, conv_b, conv_ln_w, conv_ln_b, w_out, ffn2_norm_pre, ffn2_norm_post, ffn2_w_gu, ffn2_w_down):
    n_batch, seq, d_model = x.shape
    depth = ffn1_w_gu.shape[0]
    assert d_model == D_MODEL and seq % CHUNK == 0 and (n_batch * seq) % FFN_TM == 0

    head_id = jnp.arange(GROUP) // HEAD
    e_mat = (head_id[:, None] == head_id[None, :]).astype(BF16)

    for l in range(depth):
        w_lora = jnp.zeros((D_LORA, 3 * D_RWKV), F32)
        w_lora = w_lora.at[0:LORA_W, 0:D_RWKV].set(w_up[l])
        w_lora = w_lora.at[LORA_W:LORA_W + LORA_A, D_RWKV:2 * D_RWKV].set(a_up[l])
        w_lora = w_lora.at[LORA_W + LORA_A:, 2 * D_RWKV:].set(g_up[l])
        rows = [w0[l], a0[l], k_k[l], k_a[l], r_k[l].reshape(D_RWKV), gn_w[l], gn_b[l],
                conv_b[l], conv_ln_w[l], conv_ln_b[l]]
        vec = jnp.zeros((N_VEC, D_RWKV), F32).at[0:len(rows)].set(jnp.stack(rows))
        dw = jnp.zeros((CONV_HIST, D_CONV), F32).at[0:CONV_WIDTH].set(conv_dw[l])

        x2d = _ffn(x.reshape(n_batch * seq, D_MODEL),
                   ffn1_norm_pre[l].reshape(1, D_MODEL), ffn1_norm_post[l].reshape(1, D_MODEL),
                   ffn1_w_gu[l].astype(BF16), ffn1_w_down[l].astype(BF16))
        x = _mixer(x2d.reshape(n_batch, seq, D_MODEL),
                   mix_norm_pre[l].reshape(1, D_MODEL), mix_norm_post[l].reshape(1, D_MODEL),
                   w_in[l].astype(BF16), shift_mu[l].reshape(1, D_SHIFT), w_lora.astype(BF16),
                   vec, dw, e_mat, w_out[l].astype(BF16))
        x2d = _ffn(x.reshape(n_batch * seq, D_MODEL),
                   ffn2_norm_pre[l].reshape(1, D_MODEL), ffn2_norm_post[l].reshape(1, D_MODEL),
                   ffn2_w_gu[l].astype(BF16), ffn2_w_down[l].astype(BF16))
        x = x2d.reshape(n_batch, seq, D_MODEL)
    return x
```

```python
import functools
import math

import jax
import jax.numpy as jnp
from jax import lax
from jax.experimental import pallas as pl
from jax.experimental.pallas import tpu as pltpu

F32 = jnp.float32
BF16 = jnp.bfloat16

D_MODEL = 1024
D_FF = 2816
D_RWKV = 512
D_CONV = 512
HEAD = 64
LORA_W = 64
LORA_A = 64
LORA_G = 128
D_LORA = LORA_W + LORA_A + LORA_G
D_SHIFT = 3 * D_RWKV + D_LORA
D_IN = D_SHIFT + 2 * D_CONV
CONV_WIDTH = 31
RMS_EPS = 1e-6
GN_EPS = 64e-5
LN_EPS = 1e-5
DECAY_SCALE = math.exp(-0.5)

CHUNK = 64
GROUP = 256
N_GROUPS = D_RWKV // GROUP
HEADS_PER_GROUP = GROUP // HEAD
MID = CHUNK // 2 - 1
CONV_HIST = 32
LANE_TILE = 128
assert CHUNK == HEAD

FFN_TM = 1024
FFN_FC = 256
FFN_DOWN_ROWS = 256
VMEM_LIMIT = 56 * 1024 * 1024

(V_W0, V_A0, V_KK, V_KA, V_RK, V_GNW, V_GNB, V_CB, V_LNW, V_LNB) = range(10)
N_VEC = 16


def _rmsnorm(x, g):
    ms = jnp.mean(x * x, axis=-1, keepdims=True)
    return x * lax.rsqrt(ms + RMS_EPS) * g


def _dot(a, b):
    return jnp.dot(a, b, preferred_element_type=F32)


def _dot_nt(a, b):
    return lax.dot_general(a, b, (((1,), (1,)), ((), ())), preferred_element_type=F32)


def _dot_tn(a, b):
    return lax.dot_general(a, b, (((0,), (0,)), ((), ())), preferred_element_type=F32)


def _ffn_kernel(x_ref, gpre_ref, gpost_ref, wgu_ref, wd_ref, o_ref, act_ref):
    x = x_ref[...]
    inv = lax.rsqrt(jnp.mean(x * x, axis=-1, keepdims=True) + RMS_EPS)
    h = (x * gpre_ref[...]).astype(BF16)
    for c in range(D_FF // FFN_FC):
        lo = c * FFN_FC
        gate = _dot(h, wgu_ref[:, lo:lo + FFN_FC]) * inv
        up = _dot(h, wgu_ref[:, D_FF + lo:D_FF + lo + FFN_FC]) * inv
        act_ref[:, lo:lo + FFN_FC] = (jax.nn.silu(gate) * up).astype(BF16)
    for half in range(FFN_TM // FFN_DOWN_ROWS):
        rows = slice(half * FFN_DOWN_ROWS, (half + 1) * FFN_DOWN_ROWS)
        f = _dot(act_ref[rows, :], wd_ref[...])
        o_ref[rows, :] = x_ref[rows, :] + 0.5 * _rmsnorm(f, gpost_ref[...])


def _ffn(x2d, g_pre, g_post, w_gu, w_down):
    n_tok = x2d.shape[0]
    const = lambda i: (0, 0)
    return pl.pallas_call(
        _ffn_kernel,
        out_shape=jax.ShapeDtypeStruct((n_tok, D_MODEL), F32),
        grid=(n_tok // FFN_TM,),
        in_specs=[
            pl.BlockSpec((FFN_TM, D_MODEL), lambda i: (i, 0)),
            pl.BlockSpec((1, D_MODEL), const),
            pl.BlockSpec((1, D_MODEL), const),
            pl.BlockSpec((D_MODEL, 2 * D_FF), const, pipeline_mode=pl.Buffered(1)),
            pl.BlockSpec((D_FF, D_MODEL), const, pipeline_mode=pl.Buffered(1)),
        ],
        out_specs=pl.BlockSpec((FFN_TM, D_MODEL), lambda i: (i, 0)),
        scratch_shapes=[pltpu.VMEM((FFN_TM, D_FF), BF16)],
        compiler_params=pltpu.CompilerParams(
            dimension_semantics=("arbitrary",), vmem_limit_bytes=VMEM_LIMIT),
        name="ffn",
    )(x2d, g_pre, g_post, w_gu, w_down)


def _split2(x):
    hi = x.astype(BF16)
    lo = (x - hi.astype(F32)).astype(BF16)
    return hi, lo


def _headsum(x, e_ref, split=False):
    e = e_ref[...]
    hi, lo = _split2(x) if split else (x.astype(BF16), None)
    out = []
    for g in range(N_GROUPS):
        lanes = slice(g * GROUP, (g + 1) * GROUP)
        s = _dot(hi[:, lanes], e)
        if split:
            s = s + _dot(lo[:, lanes], e)
        out.append(s)
    return jnp.concatenate(out, axis=-1)


def _mixer_kernel(x_ref, gpre_ref, gpost_ref, win_ref, mu_ref, wlora_ref, vec_ref,
                  dw_ref, e_ref, wout_ref, o_ref,
                  carry_ref, gh_ref, state_ref, *, n_batch):
    step = pl.program_id(0)
    m_rows = n_batch * CHUNK

    @pl.when(step == 0)
    def _init():
        carry_ref[...] = jnp.zeros_like(carry_ref)
        gh_ref[:, 0:CONV_HIST, :] = jnp.zeros((n_batch, CONV_HIST, D_CONV), F32)
        state_ref[...] = jnp.zeros_like(state_ref)

    vec = lambda row: vec_ref[row:row + 1, :]

    x = x_ref[...].reshape(m_rows, D_MODEL)
    h = _rmsnorm(x, gpre_ref[...]).astype(BF16)
    p = _dot(h, win_ref[...])

    ps = p[:, :D_SHIFT].reshape(n_batch, CHUNK, D_SHIFT)
    prev = pltpu.roll(ps, 1, axis=1)
    frame = lax.broadcasted_iota(jnp.int32, (1, CHUNK, 1), 1)
    prev = jnp.where(frame == 0, carry_ref[...], prev)
    carry_ref[...] = ps[:, CHUNK - 1:CHUNK, :]
    ps = (ps + (prev - ps) * mu_ref[...]).reshape(m_rows, D_SHIFT)

    r = ps[:, 0:D_RWKV]
    k = ps[:, D_RWKV:2 * D_RWKV]
    v = ps[:, 2 * D_RWKV:3 * D_RWKV]
    xl = ps[:, 3 * D_RWKV:D_SHIFT]

    lane = lax.broadcasted_iota(jnp.int32, (1, D_LORA), 1)
    lact = jnp.where(lane < LORA_W, jnp.tanh(xl),
                     jnp.where(lane < LORA_W + LORA_A, xl, jax.nn.sigmoid(xl)))
    lora = _dot(lact.astype(BF16), wlora_ref[...])
    lw = -DECAY_SCALE * jax.nn.sigmoid(vec(V_W0) + lora[:, 0:D_RWKV])
    a = jax.nn.sigmoid(vec(V_A0) + lora[:, D_RWKV:2 * D_RWKV])
    gate = lora[:, 2 * D_RWKV:3 * D_RWKV]

    kk = k * vec(V_KK)
    kk = kk * lax.rsqrt(jnp.maximum(_headsum(kk * kk, e_ref), 1e-12))
    k2 = k * (1.0 + (a - 1.0) * vec(V_KA))
    bonus = _headsum(r * k2 * vec(V_RK), e_ref) * v
    z = -kk
    b = kk * a

    row_t = lax.broadcasted_iota(jnp.int32, (CHUNK, GROUP), 0)
    col_s = lax.broadcasted_iota(jnp.int32, (CHUNK, GROUP), 1) % CHUNK
    strict = col_s < row_t
    incl = col_s <= row_t
    eye = (col_s == row_t).astype(F32)
    lt_r = lax.broadcasted_iota(jnp.int32, (CHUNK, CHUNK), 0)
    lt_c = lax.broadcasted_iota(jnp.int32, (CHUNK, CHUNK), 1)
    lower = (lt_c <= lt_r).astype(BF16)

    half_id = lax.broadcasted_iota(jnp.int32, (CHUNK, LANE_TILE), 1) // HEAD
    in_half = [half_id == 0, half_id == 1]
    half_keep = [m.astype(BF16) for m in in_half]
    tile_of = lambda hd: slice(LANE_TILE * (hd // 2), LANE_TILE * (hd // 2 + 1))
    zero_tile = jnp.zeros((CHUNK, LANE_TILE), BF16)

    def place(tiles):
        return jnp.concatenate(
            [jnp.concatenate([tiles[hd] if j == hd // 2 else zero_tile
                              for j in range(GROUP // LANE_TILE)], axis=1)
             for hd in range(HEADS_PER_GROUP)], axis=0)

    def bd(t):
        tb = t.astype(BF16)
        return place([tb[:, tile_of(hd)] * half_keep[hd % 2] for hd in range(HEADS_PER_GROUP)])

    chains = [(bi, g) for bi in range(n_batch) for g in range(N_GROUPS)]
    stage = lambda fn, *lists: [fn(*args) for args in zip(*lists)]

    def cumulate(bi):
        lw_b = lw[bi * CHUNK:(bi + 1) * CHUNK, :]
        hi, lo = _split2(lw_b)
        return _dot(lower, hi) + _dot(lower, lo)
    cums = [cumulate(bi) for bi in range(n_batch)]

    def prepare(chain):
        bi, g = chain
        rows = slice(bi * CHUNK, (bi + 1) * CHUNK)
        lanes = slice(g * GROUP, (g + 1) * GROUP)
        lw_g = lw[rows, lanes]
        cl = cums[bi][:, lanes]
        ref = cl[MID:MID + 1, :]
        e_out = jnp.exp(ref - cl)
        rt = (r[rows, lanes] * jnp.exp(cl - ref)).astype(BF16)
        zt = (z[rows, lanes] * jnp.exp(cl - lw_g - ref)).astype(BF16)
        kt = k2[rows, lanes] * e_out
        bt = b[rows, lanes] * e_out
        vg = v[rows, lanes]
        st = state_ref[bi, g]
        e_ref0 = jnp.exp(ref)
        s0 = [st[HEAD * hd:HEAD * (hd + 1), :] * e_ref0[:, tile_of(hd)]
              for hd in range(HEADS_PER_GROUP)]
        decay_end = jnp.exp(cl[CHUNK - 1:CHUNK, :] - ref)
        return dict(rt=rt, zt=zt, kt=kt, bt=bt, vg=vg, s0=s0,
                    s0b=place([s.astype(BF16) for s in s0]), decay_end=decay_end)
    c = stage(prepare, chains)

    def a_matrices(ci):
        zr = jnp.concatenate([ci["zt"], ci["rt"]], axis=0)
        a_b = _dot_nt(zr, bd(ci["bt"]))
        a_k = _dot_nt(zr, bd(ci["kt"]))
        a_bz = jnp.where(strict, a_b[:CHUNK], 0.0)
        return dict(ci, a_bz=a_bz,
                    a_br=jnp.where(incl, a_b[CHUNK:], 0.0).astype(BF16),
                    a_kz=jnp.where(strict, a_k[:CHUNK], 0.0).astype(BF16),
                    a_kr=jnp.where(incl, a_k[CHUNK:], 0.0).astype(BF16))
    c = stage(a_matrices, c)

    t_acc = stage(lambda ci: eye + ci["a_bz"], c)
    pw = stage(lambda ci: _dot(ci["a_bz"].astype(BF16), bd(ci["a_bz"])), c)
    n = 2
    while n < CHUNK // 2:
        res = stage(lambda t, q: _dot(jnp.concatenate([t, q], axis=0).astype(BF16), bd(q)), t_acc, pw)
        t_acc = stage(lambda t, rr: t + rr[:CHUNK], t_acc, res)
        pw = stage(lambda rr: rr[CHUNK:], res)
        n *= 2
    t_acc = stage(lambda t, q: t + _dot(t.astype(BF16), bd(q)), t_acc, pw)

    vbd = stage(lambda ci: bd(ci["vg"]), c)
    rhs0 = stage(lambda ci, vb: _dot_nt(ci["zt"], ci["s0b"]) + _dot(ci["a_kz"], vb), c, vbd)
    u = stage(lambda t, q: _dot(t.astype(BF16), bd(q)), t_acc, rhs0)
    y = stage(lambda ci, ui, vb: _dot_nt(ci["rt"], ci["s0b"]) + _dot(ci["a_br"], bd(ui))
              + _dot(ci["a_kr"], vb), c, u, vbd)
    upd = stage(lambda ci, ui: _dot_tn(jnp.concatenate([ui, ci["vg"]], axis=0).astype(BF16),
                                       jnp.concatenate([ci["bt"], ci["kt"]], axis=0).astype(BF16)), c, u)
    for (bi, g), ci, di in zip(chains, c, upd):
        state_ref[bi, g] = jnp.concatenate(
            [(ci["s0"][hd] + jnp.where(in_half[hd % 2], di[HEAD * hd:HEAD * (hd + 1), tile_of(hd)], 0.0))
             * ci["decay_end"][:, tile_of(hd)] for hd in range(HEADS_PER_GROUP)], axis=0)

    y = jnp.concatenate([jnp.concatenate(y[N_GROUPS * bi:N_GROUPS * (bi + 1)], axis=1)
                         for bi in range(n_batch)], axis=0)
    mean = _headsum(y, e_ref, split=True) * (1.0 / HEAD)
    yc = y - mean
    var = _headsum(yc * yc, e_ref) * (1.0 / HEAD)
    yn = yc * lax.rsqrt(var + GN_EPS) * vec(V_GNW) + vec(V_GNB)
    out_a = (yn + bonus) * gate

    pc = p[:, D_SHIFT:]
    glu = pc[:, :D_CONV] * jax.nn.sigmoid(pc[:, D_CONV:])
    gh_ref[:, CONV_HIST:CONV_HIST + CHUNK, :] = glu.reshape(n_batch, CHUNK, D_CONV)
    conv_rows = []
    for bi in range(n_batch):
        gh = gh_ref[bi]
        acc = jnp.zeros((CHUNK, D_CONV), F32) + vec(V_CB)
        for rr in range(8):
            shifted = pltpu.roll(gh, rr, axis=0) if rr else gh
            for q in range(CONV_HIST // 8):
                s = 8 * q + rr
                if s >= CONV_WIDTH:
                    continue
                w = CONV_WIDTH - 1 - s
                lo = CONV_HIST - 8 * q
                acc = acc + shifted[lo:lo + CHUNK, :] * dw_ref[w:w + 1, :]
        conv_rows.append(acc)
    gh_ref[:, 0:CONV_HIST, :] = gh_ref[:, CHUNK:CHUNK + CONV_HIST, :]
    cv = jnp.concatenate(conv_rows, axis=0)
    c_mean = jnp.mean(cv, axis=-1, keepdims=True)
    cc = cv - c_mean
    c_var = jnp.mean(cc * cc, axis=-1, keepdims=True)
    out_b = jax.nn.silu(cc * lax.rsqrt(c_var + LN_EPS) * vec(V_LNW) + vec(V_LNB))

    m = (_dot(out_a.astype(BF16), wout_ref[0:D_RWKV, :])
         + _dot(out_b.astype(BF16), wout_ref[D_RWKV:, :]))
    o_ref[...] = (x + _rmsnorm(m, gpost_ref[...])).reshape(n_batch, CHUNK, D_MODEL)


def _mixer(x, g_pre, g_post, w_in, mu, w_lora, vec, dw, e_mat, w_out):
    n_batch, seq, _ = x.shape
    const = lambda i: (0, 0)
    one = dict(pipeline_mode=pl.Buffered(1))
    return pl.pallas_call(
        functools.partial(_mixer_kernel, n_batch=n_batch),
        out_shape=jax.ShapeDtypeStruct(x.shape, F32),
        grid=(seq // CHUNK,),
        in_specs=[
            pl.BlockSpec((n_batch, CHUNK, D_MODEL), lambda i: (0, i, 0)),
            pl.BlockSpec((1, D_MODEL), const),
            pl.BlockSpec((1, D_MODEL), const),
            pl.BlockSpec((D_MODEL, D_IN), const, **one),
            pl.BlockSpec((1, D_SHIFT), const),
            pl.BlockSpec((D_LORA, 3 * D_RWKV), const, **one),
            pl.BlockSpec((N_VEC, D_RWKV), const),
            pl.BlockSpec((CONV_HIST, D_CONV), const),
            pl.BlockSpec((GROUP, GROUP), const, **one),
            pl.BlockSpec((D_MODEL, D_MODEL), const, **one),
        ],
        out_specs=pl.BlockSpec((n_batch, CHUNK, D_MODEL), lambda i: (0, i, 0)),
        scratch_shapes=[
            pltpu.VMEM((n_batch, 1, D_SHIFT), F32),
            pltpu.VMEM((n_batch, CONV_HIST + CHUNK, D_CONV), F32),
            pltpu.VMEM((n_batch, N_GROUPS, GROUP, LANE_TILE), F32),
        ],
        compiler_params=pltpu.CompilerParams(
            dimension_semantics=("arbitrary",), vmem_limit_bytes=VMEM_LIMIT),
        name="mixer",
    )(x, g_pre, g_post, w_in, mu, w_lora, vec, dw, e_mat, w_out)


def kernel(x, ffn1_norm_pre, ffn1_norm_post, ffn1_w_gu, ffn1_w_down, mix_norm_pre, mix_norm_post, w_in, shift_mu, w_up, w0, a_up, a0, g_up, k_k, k_a, r_k, gn_w, gn_b, conv_dw, conv_b, conv_ln_w, conv_ln_b, w_out, ffn2_norm_pre, ffn2_norm_post, ffn2_w_gu, ffn2_w_down):
    n_batch, seq, d_model = x.shape
    depth = ffn1_w_gu.shape[0]
    assert d_model == D_MODEL and seq % CHUNK == 0 and (n_batch * seq) % FFN_TM == 0

    head_id = jnp.arange(GROUP) // HEAD
    e_mat = (head_id[:, None] == head_id[None, :]).astype(BF16)

    for l in range(depth):
        w_lora = jnp.zeros((D_LORA, 3 * D_RWKV), F32)
        w_lora = w_lora.at[0:LORA_W, 0:D_RWKV].set(w_up[l])
        w_lora = w_lora.at[LORA_W:LORA_W + LORA_A, D_RWKV:2 * D_RWKV].set(a_up[l])
        w_lora = w_lora.at[LORA_W + LORA_A:, 2 * D_RWKV:].set(g_up[l])
        rows = [w0[l], a0[l], k_k[l], k_a[l], r_k[l].reshape(D_RWKV), gn_w[l], gn_b[l],
                conv_b[l], conv_ln_w[l], conv_ln_b[l]]
        vec = jnp.zeros((N_VEC, D_RWKV), F32).at[0:len(rows)].set(jnp.stack(rows))
        dw = jnp.zeros((CONV_HIST, D_CONV), F32).at[0:CONV_WIDTH].set(conv_dw[l])

        x2d = _ffn(x.reshape(n_batch * seq, D_MODEL),
                   ffn1_norm_pre[l].reshape(1, D_MODEL), ffn1_norm_post[l].reshape(1, D_MODEL),
                   ffn1_w_gu[l].astype(BF16), ffn1_w_down[l].astype(BF16))
        x = _mixer(x2d.reshape(n_batch, seq, D_MODEL),
                   mix_norm_pre[l].reshape(1, D_MODEL), mix_norm_post[l].reshape(1, D_MODEL),
                   w_in[l].astype(BF16), shift_mu[l].reshape(1, D_SHIFT), w_lora.astype(BF16),
                   vec, dw, e_mat, w_out[l].astype(BF16))
        x2d = _ffn(x.reshape(n_batch * seq, D_MODEL),
                   ffn2_norm_pre[l].reshape(1, D_MODEL), ffn2_norm_post[l].reshape(1, D_MODEL),
                   ffn2_w_gu[l].astype(BF16), ffn2_w_down[l].astype(BF16))
        x = x2d.reshape(n_batch, seq, D_MODEL)
    return x
```

```python
import functools
import math

import jax
import jax.numpy as jnp
from jax import lax
from jax.experimental import pallas as pl
from jax.experimental.pallas import tpu as pltpu

F32 = jnp.float32
BF16 = jnp.bfloat16

D_MODEL = 1024
D_FF = 2816
D_RWKV = 512
D_CONV = 512
HEAD = 64
LORA_W = 64
LORA_A = 64
LORA_G = 128
D_LORA = LORA_W + LORA_A + LORA_G
D_SHIFT = 3 * D_RWKV + D_LORA
D_IN = D_SHIFT + 2 * D_CONV
CONV_WIDTH = 31
RMS_EPS = 1e-6
GN_EPS = 64e-5
LN_EPS = 1e-5
DECAY_SCALE = math.exp(-0.5)

CHUNK = 64
GROUP = 256
N_GROUPS = D_RWKV // GROUP
HEADS_PER_GROUP = GROUP // HEAD
MID = CHUNK // 2 - 1
CONV_HIST = 32
LANE_TILE = 128
SUBLANES = 8
WAVE = 8
assert CHUNK == HEAD

FFN_TM = 1024
FFN_FC = 256
FFN_DOWN_ROWS = 256
VMEM_LIMIT = 56 * 1024 * 1024

(V_W0, V_A0, V_KK, V_KA, V_RK, V_GNW, V_GNB, V_CB, V_LNW, V_LNB) = range(10)
N_VEC = 16


def _rmsnorm(x, g):
    ms = jnp.mean(x * x, axis=-1, keepdims=True)
    return x * lax.rsqrt(ms + RMS_EPS) * g


def _dot(a, b):
    return jnp.dot(a, b, preferred_element_type=F32)


def _dot_nt(a, b):
    return lax.dot_general(a, b, (((1,), (1,)), ((), ())), preferred_element_type=F32)


def _dot_tn(a, b):
    return lax.dot_general(a, b, (((0,), (0,)), ((), ())), preferred_element_type=F32)


def _ffn_kernel(x_ref, gpre_ref, gpost_ref, wgu_ref, wd_ref, o_ref, act_ref):
    x = x_ref[...]
    inv = lax.rsqrt(jnp.mean(x * x, axis=-1, keepdims=True) + RMS_EPS)
    h = (x * gpre_ref[...]).astype(BF16)
    for c in range(D_FF // FFN_FC):
        lo = c * FFN_FC
        gate = _dot(h, wgu_ref[:, lo:lo + FFN_FC]) * inv
        up = _dot(h, wgu_ref[:, D_FF + lo:D_FF + lo + FFN_FC]) * inv
        act_ref[:, lo:lo + FFN_FC] = (jax.nn.silu(gate) * up).astype(BF16)
    for half in range(FFN_TM // FFN_DOWN_ROWS):
        rows = slice(half * FFN_DOWN_ROWS, (half + 1) * FFN_DOWN_ROWS)
        f = _dot(act_ref[rows, :], wd_ref[...])
        o_ref[rows, :] = x_ref[rows, :] + 0.5 * _rmsnorm(f, gpost_ref[...])


def _ffn(x2d, g_pre, g_post, w_gu, w_down):
    n_tok = x2d.shape[0]
    const = lambda i: (0, 0)
    return pl.pallas_call(
        _ffn_kernel,
        out_shape=jax.ShapeDtypeStruct((n_tok, D_MODEL), F32),
        grid=(n_tok // FFN_TM,),
        in_specs=[
            pl.BlockSpec((FFN_TM, D_MODEL), lambda i: (i, 0)),
            pl.BlockSpec((1, D_MODEL), const),
            pl.BlockSpec((1, D_MODEL), const),
            pl.BlockSpec((D_MODEL, 2 * D_FF), const, pipeline_mode=pl.Buffered(1)),
            pl.BlockSpec((D_FF, D_MODEL), const, pipeline_mode=pl.Buffered(1)),
        ],
        out_specs=pl.BlockSpec((FFN_TM, D_MODEL), lambda i: (i, 0)),
        scratch_shapes=[pltpu.VMEM((FFN_TM, D_FF), BF16)],
        compiler_params=pltpu.CompilerParams(
            dimension_semantics=("arbitrary",), vmem_limit_bytes=VMEM_LIMIT),
        name="ffn",
    )(x2d, g_pre, g_post, w_gu, w_down)


def _split2(x):
    hi = x.astype(BF16)
    lo = (x - hi.astype(F32)).astype(BF16)
    return hi, lo


def _headsum(x, e_ref, split=False):
    e = e_ref[...]
    hi, lo = _split2(x) if split else (x.astype(BF16), None)
    out = []
    for g in range(N_GROUPS):
        lanes = slice(g * GROUP, (g + 1) * GROUP)
        s = _dot(hi[:, lanes], e)
        if split:
            s = s + _dot(lo[:, lanes], e)
        out.append(s)
    return jnp.concatenate(out, axis=-1)


def _mixer_kernel(x_ref, gpre_ref, gpost_ref, win_ref, mu_ref, wlora_ref, vec_ref,
                  dw_ref, e_ref, wout_ref, o_ref,
                  carry_ref, gh_ref, state_ref, *, n_batch):
    step = pl.program_id(0)
    m_rows = n_batch * CHUNK

    @pl.when(step == 0)
    def _init():
        carry_ref[...] = jnp.zeros_like(carry_ref)
        gh_ref[:, 0:CONV_HIST, :] = jnp.zeros((n_batch, CONV_HIST, D_CONV), F32)
        state_ref[...] = jnp.zeros_like(state_ref)

    vec = lambda row: vec_ref[row:row + 1, :]

    x = x_ref[...].reshape(m_rows, D_MODEL)
    h = _rmsnorm(x, gpre_ref[...]).astype(BF16)
    p = _dot(h, win_ref[...])

    ps = p[:, :D_SHIFT].reshape(n_batch, CHUNK, D_SHIFT)
    prev = pltpu.roll(ps, 1, axis=1)
    frame = lax.broadcasted_iota(jnp.int32, (1, CHUNK, 1), 1)
    prev = jnp.where(frame == 0, carry_ref[...], prev)
    carry_ref[...] = ps[:, CHUNK - 1:CHUNK, :]
    ps = (ps + (prev - ps) * mu_ref[...]).reshape(m_rows, D_SHIFT)

    r = ps[:, 0:D_RWKV]
    k = ps[:, D_RWKV:2 * D_RWKV]
    v = ps[:, 2 * D_RWKV:3 * D_RWKV]
    xl = ps[:, 3 * D_RWKV:D_SHIFT]

    lane = lax.broadcasted_iota(jnp.int32, (1, D_LORA), 1)
    lact = jnp.where(lane < LORA_W, jnp.tanh(xl),
                     jnp.where(lane < LORA_W + LORA_A, xl, jax.nn.sigmoid(xl)))
    lora = _dot(lact.astype(BF16), wlora_ref[...])
    lw = -DECAY_SCALE * jax.nn.sigmoid(vec(V_W0) + lora[:, 0:D_RWKV])
    a = jax.nn.sigmoid(vec(V_A0) + lora[:, D_RWKV:2 * D_RWKV])
    gate = lora[:, 2 * D_RWKV:3 * D_RWKV]

    kk = k * vec(V_KK)
    kk = kk * lax.rsqrt(jnp.maximum(_headsum(kk * kk, e_ref), 1e-12))
    k2 = k * (1.0 + (a - 1.0) * vec(V_KA))
    bonus = _headsum(r * k2 * vec(V_RK), e_ref) * v
    z = -kk
    b = kk * a

    row_t = lax.broadcasted_iota(jnp.int32, (CHUNK, GROUP), 0)
    col_s = lax.broadcasted_iota(jnp.int32, (CHUNK, GROUP), 1) % CHUNK
    strict = col_s < row_t
    incl = col_s <= row_t
    eye = (col_s == row_t).astype(F32)
    lt_r = lax.broadcasted_iota(jnp.int32, (CHUNK, CHUNK), 0)
    lt_c = lax.broadcasted_iota(jnp.int32, (CHUNK, CHUNK), 1)
    lower = (lt_c <= lt_r).astype(BF16)

    half_id = lax.broadcasted_iota(jnp.int32, (CHUNK, LANE_TILE), 1) // HEAD
    in_half = [half_id == 0, half_id == 1]
    half_keep = [m.astype(BF16) for m in in_half]
    tile_of = lambda hd: slice(LANE_TILE * (hd // 2), LANE_TILE * (hd // 2 + 1))
    zero_tile = jnp.zeros((CHUNK, LANE_TILE), BF16)

    def place(tiles):
        return jnp.concatenate(
            [jnp.concatenate([tiles[hd] if j == hd // 2 else zero_tile
                              for j in range(GROUP // LANE_TILE)], axis=1)
             for hd in range(HEADS_PER_GROUP)], axis=0)

    def bd(t):
        tb = t.astype(BF16)
        return place([tb[:, tile_of(hd)] * half_keep[hd % 2] for hd in range(HEADS_PER_GROUP)])

    chains = [(bi, g) for bi in range(n_batch) for g in range(N_GROUPS)]
    stage = lambda fn, *lists: [fn(*args) for args in zip(*lists)]

    def cumulate(bi):
        lw_b = lw[bi * CHUNK:(bi + 1) * CHUNK, :]
        hi, lo = _split2(lw_b)
        return _dot(lower, hi) + _dot(lower, lo)
    cums = [cumulate(bi) for bi in range(n_batch)]

    def prepare(chain):
        bi, g = chain
        rows = slice(bi * CHUNK, (bi + 1) * CHUNK)
        lanes = slice(g * GROUP, (g + 1) * GROUP)
        lw_g = lw[rows, lanes]
        cl = cums[bi][:, lanes]
        ref = cl[MID:MID + 1, :]
        e_out = jnp.exp(ref - cl)
        rt = (r[rows, lanes] * jnp.exp(cl - ref)).astype(BF16)
        zt = (z[rows, lanes] * jnp.exp(cl - lw_g - ref)).astype(BF16)
        kt = k2[rows, lanes] * e_out
        bt = b[rows, lanes] * e_out
        vg = v[rows, lanes]
        st = state_ref[bi, g]
        e_ref0 = jnp.exp(ref)
        s0 = [st[HEAD * hd:HEAD * (hd + 1), :] * e_ref0[:, tile_of(hd)]
              for hd in range(HEADS_PER_GROUP)]
        decay_end = jnp.exp(cl[CHUNK - 1:CHUNK, :] - ref)
        return dict(rt=rt, zt=zt, kt=kt, bt=bt, vg=vg, s0=s0,
                    s0b=place([s.astype(BF16) for s in s0]), decay_end=decay_end)
    def run_wave(wave):
        c = stage(prepare, wave)

        def a_matrices(ci):
            zr = jnp.concatenate([ci["zt"], ci["rt"]], axis=0)
            a_b = _dot_nt(zr, bd(ci["bt"]))
            a_k = _dot_nt(zr, bd(ci["kt"]))
            a_bz = jnp.where(strict, a_b[:CHUNK], 0.0)
            return dict(ci, a_bz=a_bz,
                        a_br=jnp.where(incl, a_b[CHUNK:], 0.0).astype(BF16),
                        a_kz=jnp.where(strict, a_k[:CHUNK], 0.0).astype(BF16),
                        a_kr=jnp.where(incl, a_k[CHUNK:], 0.0).astype(BF16))
        c = stage(a_matrices, c)

        t_acc = stage(lambda ci: eye + ci["a_bz"], c)
        pw = stage(lambda ci: _dot(ci["a_bz"].astype(BF16), bd(ci["a_bz"])), c)
        n = 2
        while n < CHUNK // 2:
            res = stage(lambda t, q: _dot(jnp.concatenate([t, q], axis=0).astype(BF16), bd(q)), t_acc, pw)
            t_acc = stage(lambda t, rr: t + rr[:CHUNK], t_acc, res)
            pw = stage(lambda rr: rr[CHUNK:], res)
            n *= 2
        t_acc = stage(lambda t, q: t + _dot(t.astype(BF16), bd(q)), t_acc, pw)

        vbd = stage(lambda ci: bd(ci["vg"]), c)
        rhs0 = stage(lambda ci, vb: _dot_nt(ci["zt"], ci["s0b"]) + _dot(ci["a_kz"], vb), c, vbd)
        u = stage(lambda t, q: _dot(t.astype(BF16), bd(q)), t_acc, rhs0)
        y = stage(lambda ci, ui, vb: _dot_nt(ci["rt"], ci["s0b"]) + _dot(ci["a_br"], bd(ui))
                  + _dot(ci["a_kr"], vb), c, u, vbd)
        upd = stage(lambda ci, ui: _dot_tn(jnp.concatenate([ui, ci["vg"]], axis=0).astype(BF16),
                                           jnp.concatenate([ci["bt"], ci["kt"]], axis=0).astype(BF16)), c, u)
        for (bi, g), ci, di in zip(wave, c, upd):
            state_ref[bi, g] = jnp.concatenate(
                [(ci["s0"][hd] + jnp.where(in_half[hd % 2], di[HEAD * hd:HEAD * (hd + 1), tile_of(hd)], 0.0))
                 * ci["decay_end"][:, tile_of(hd)] for hd in range(HEADS_PER_GROUP)], axis=0)
        return y

    y = []
    for first in range(0, len(chains), WAVE):
        y += run_wave(chains[first:first + WAVE])

    y = jnp.concatenate([jnp.concatenate(y[N_GROUPS * bi:N_GROUPS * (bi + 1)], axis=1)
                         for bi in range(n_batch)], axis=0)
    mean = _headsum(y, e_ref, split=True) * (1.0 / HEAD)
    yc = y - mean
    var = _headsum(yc * yc, e_ref) * (1.0 / HEAD)
    yn = yc * lax.rsqrt(var + GN_EPS) * vec(V_GNW) + vec(V_GNB)
    out_a = (yn + bonus) * gate

    pc = p[:, D_SHIFT:]
    glu = pc[:, :D_CONV] * jax.nn.sigmoid(pc[:, D_CONV:])
    gh_ref[:, CONV_HIST:CONV_HIST + CHUNK, :] = glu.reshape(n_batch, CHUNK, D_CONV)
    conv_rows = []
    for bi in range(n_batch):
        gh = gh_ref[bi]
        acc = jnp.zeros((CHUNK, D_CONV), F32) + vec(V_CB)
        for rr in range(SUBLANES):
            shifted = pltpu.roll(gh, rr, axis=0) if rr else gh
            for q in range(CONV_HIST // SUBLANES):
                s = SUBLANES * q + rr
                if s >= CONV_WIDTH:
                    continue
                w = CONV_WIDTH - 1 - s
                lo = CONV_HIST - SUBLANES * q
                acc = acc + shifted[lo:lo + CHUNK, :] * dw_ref[w:w + 1, :]
        conv_rows.append(acc)
    gh_ref[:, 0:CONV_HIST, :] = gh_ref[:, CHUNK:CHUNK + CONV_HIST, :]
    cv = jnp.concatenate(conv_rows, axis=0)
    c_mean = jnp.mean(cv, axis=-1, keepdims=True)
    cc = cv - c_mean
    c_var = jnp.mean(cc * cc, axis=-1, keepdims=True)
    out_b = jax.nn.silu(cc * lax.rsqrt(c_var + LN_EPS) * vec(V_LNW) + vec(V_LNB))

    m = (_dot(out_a.astype(BF16), wout_ref[0:D_RWKV, :])
         + _dot(out_b.astype(BF16), wout_ref[D_RWKV:, :]))
    o_ref[...] = (x + _rmsnorm(m, gpost_ref[...])).reshape(n_batch, CHUNK, D_MODEL)


def _mixer(x, g_pre, g_post, w_in, mu, w_lora, vec, dw, e_mat, w_out):
    n_batch, seq, _ = x.shape
    const = lambda i: (0, 0)
    one = dict(pipeline_mode=pl.Buffered(1))
    return pl.pallas_call(
        functools.partial(_mixer_kernel, n_batch=n_batch),
        out_shape=jax.ShapeDtypeStruct(x.shape, F32),
        grid=(seq // CHUNK,),
        in_specs=[
            pl.BlockSpec((n_batch, CHUNK, D_MODEL), lambda i: (0, i, 0)),
            pl.BlockSpec((1, D_MODEL), const),
            pl.BlockSpec((1, D_MODEL), const),
            pl.BlockSpec((D_MODEL, D_IN), const, **one),
            pl.BlockSpec((1, D_SHIFT), const),
            pl.BlockSpec((D_LORA, 3 * D_RWKV), const, **one),
            pl.BlockSpec((N_VEC, D_RWKV), const),
            pl.BlockSpec((CONV_HIST, D_CONV), const),
            pl.BlockSpec((GROUP, GROUP), const, **one),
            pl.BlockSpec((D_MODEL, D_MODEL), const, **one),
        ],
        out_specs=pl.BlockSpec((n_batch, CHUNK, D_MODEL), lambda i: (0, i, 0)),
        scratch_shapes=[
            pltpu.VMEM((n_batch, 1, D_SHIFT), F32),
            pltpu.VMEM((n_batch, CONV_HIST + CHUNK, D_CONV), F32),
            pltpu.VMEM((n_batch, N_GROUPS, GROUP, LANE_TILE), F32),
        ],
        compiler_params=pltpu.CompilerParams(
            dimension_semantics=("arbitrary",), vmem_limit_bytes=VMEM_LIMIT),
        name="mixer",
    )(x, g_pre, g_post, w_in, mu, w_lora, vec, dw, e_mat, w_out)


def kernel(x, ffn1_norm_pre, ffn1_norm_post, ffn1_w_gu, ffn1_w_down, mix_norm_pre, mix_norm_post, w_in, shift_mu, w_up, w0, a_up, a0, g_up, k_k, k_a, r_k, gn_w, gn_b, conv_dw, conv_b, conv_ln_w, conv_ln_b, w_out, ffn2_norm_pre, ffn2_norm_post, ffn2_w_gu, ffn2_w_down):
    n_batch, seq, d_model = x.shape
    depth = ffn1_w_gu.shape[0]
    assert d_model == D_MODEL and seq % CHUNK == 0 and (n_batch * seq) % FFN_TM == 0

    head_id = jnp.arange(GROUP) // HEAD
    e_mat = (head_id[:, None] == head_id[None, :]).astype(BF16)

    for l in range(depth):
        w_lora = jnp.zeros((D_LORA, 3 * D_RWKV), F32)
        w_lora = w_lora.at[0:LORA_W, 0:D_RWKV].set(w_up[l])
        w_lora = w_lora.at[LORA_W:LORA_W + LORA_A, D_RWKV:2 * D_RWKV].set(a_up[l])
        w_lora = w_lora.at[LORA_W + LORA_A:, 2 * D_RWKV:].set(g_up[l])
        rows = [w0[l], a0[l], k_k[l], k_a[l], r_k[l].reshape(D_RWKV), gn_w[l], gn_b[l],
                conv_b[l], conv_ln_w[l], conv_ln_b[l]]
        vec = jnp.zeros((N_VEC, D_RWKV), F32).at[0:len(rows)].set(jnp.stack(rows))
        dw = jnp.zeros((CONV_HIST, D_CONV), F32).at[0:CONV_WIDTH].set(conv_dw[l])

        x2d = _ffn(x.reshape(n_batch * seq, D_MODEL),
                   ffn1_norm_pre[l].reshape(1, D_MODEL), ffn1_norm_post[l].reshape(1, D_MODEL),
                   ffn1_w_gu[l].astype(BF16), ffn1_w_down[l].astype(BF16))
        x = _mixer(x2d.reshape(n_batch, seq, D_MODEL),
                   mix_norm_pre[l].reshape(1, D_MODEL), mix_norm_post[l].reshape(1, D_MODEL),
                   w_in[l].astype(BF16), shift_mu[l].reshape(1, D_SHIFT), w_lora.astype(BF16),
                   vec, dw, e_mat, w_out[l].astype(BF16))
        x2d = _ffn(x.reshape(n_batch * seq, D_MODEL),
                   ffn2_norm_pre[l].reshape(1, D_MODEL), ffn2_norm_post[l].reshape(1, D_MODEL),
                   ffn2_w_gu[l].astype(BF16), ffn2_w_down[l].astype(BF16))
        x = x2d.reshape(n_batch, seq, D_MODEL)
    return x
```

```python
import functools
import math

import jax
import jax.numpy as jnp
from jax import lax
from jax.experimental import pallas as pl
from jax.experimental.pallas import tpu as pltpu

F32 = jnp.float32
BF16 = jnp.bfloat16

D_MODEL = 1024
D_FF = 2816
D_RWKV = 512
D_CONV = 512
HEAD = 64
LORA_W = 64
LORA_A = 64
LORA_G = 128
D_LORA = LORA_W + LORA_A + LORA_G
D_SHIFT = 3 * D_RWKV + D_LORA
D_IN = D_SHIFT + 2 * D_CONV
CONV_WIDTH = 31
RMS_EPS = 1e-6
GN_EPS = 64e-5
LN_EPS = 1e-5
DECAY_SCALE = math.exp(-0.5)

CHUNK = 64
GROUP = 256
N_GROUPS = D_RWKV // GROUP
HEADS_PER_GROUP = GROUP // HEAD
MID = CHUNK // 2 - 1
CONV_HIST = 32
LANE_TILE = 128
SUBLANES = 8
WAVE = 8
assert CHUNK == HEAD

FFN_TM = 1024
FFN_FC = 256
FFN_DOWN_ROWS = 256
VMEM_LIMIT = 56 * 1024 * 1024

(V_W0, V_A0, V_KK, V_KA, V_RK, V_GNW, V_GNB, V_CB, V_LNW, V_LNB) = range(10)
N_VEC = 16


def _rmsnorm(x, g):
    ms = jnp.mean(x * x, axis=-1, keepdims=True)
    return x * lax.rsqrt(ms + RMS_EPS) * g


def _dot(a, b):
    return jnp.dot(a, b, preferred_element_type=F32)


def _dot_nt(a, b):
    return lax.dot_general(a, b, (((1,), (1,)), ((), ())), preferred_element_type=F32)


def _dot_tn(a, b):
    return lax.dot_general(a, b, (((0,), (0,)), ((), ())), preferred_element_type=F32)


def _ffn_kernel(x_ref, gpre_ref, gpost_ref, wgu_ref, wd_ref, o_ref, act_ref):
    x = x_ref[...]
    inv = lax.rsqrt(jnp.mean(x * x, axis=-1, keepdims=True) + RMS_EPS)
    h = (x * gpre_ref[...]).astype(BF16)
    for c in range(D_FF // FFN_FC):
        lo = c * FFN_FC
        gate = _dot(h, wgu_ref[:, lo:lo + FFN_FC]) * inv
        up = _dot(h, wgu_ref[:, D_FF + lo:D_FF + lo + FFN_FC]) * inv
        act_ref[:, lo:lo + FFN_FC] = (jax.nn.silu(gate) * up).astype(BF16)
    for blk in range(FFN_TM // FFN_DOWN_ROWS):
        rows = slice(blk * FFN_DOWN_ROWS, (blk + 1) * FFN_DOWN_ROWS)
        f = _dot(act_ref[rows, :], wd_ref[...])
        o_ref[rows, :] = x_ref[rows, :] + 0.5 * _rmsnorm(f, gpost_ref[...])


def _ffn(x2d, g_pre, g_post, w_gu, w_down):
    n_tok = x2d.shape[0]
    const = lambda i: (0, 0)
    return pl.pallas_call(
        _ffn_kernel,
        out_shape=jax.ShapeDtypeStruct((n_tok, D_MODEL), F32),
        grid=(n_tok // FFN_TM,),
        in_specs=[
            pl.BlockSpec((FFN_TM, D_MODEL), lambda i: (i, 0)),
            pl.BlockSpec((1, D_MODEL), const),
            pl.BlockSpec((1, D_MODEL), const),
            pl.BlockSpec((D_MODEL, 2 * D_FF), const, pipeline_mode=pl.Buffered(1)),
            pl.BlockSpec((D_FF, D_MODEL), const, pipeline_mode=pl.Buffered(1)),
        ],
        out_specs=pl.BlockSpec((FFN_TM, D_MODEL), lambda i: (i, 0)),
        scratch_shapes=[pltpu.VMEM((FFN_TM, D_FF), BF16)],
        compiler_params=pltpu.CompilerParams(
            dimension_semantics=("arbitrary",), vmem_limit_bytes=VMEM_LIMIT),
        name="ffn",
    )(x2d, g_pre, g_post, w_gu, w_down)


def _split2(x):
    hi = x.astype(BF16)
    lo = (x - hi.astype(F32)).astype(BF16)
    return hi, lo


def _headsum(x, e_ref, split=False):
    e = e_ref[...]
    hi, lo = _split2(x) if split else (x.astype(BF16), None)
    out = []
    for g in range(N_GROUPS):
        lanes = slice(g * GROUP, (g + 1) * GROUP)
        s = _dot(hi[:, lanes], e)
        if split:
            s = s + _dot(lo[:, lanes], e)
        out.append(s)
    return jnp.concatenate(out, axis=-1)


def _mixer_kernel(x_ref, gpre_ref, gpost_ref, win_ref, mu_ref, wlora_ref, vec_ref,
                  dw_ref, e_ref, wout_ref, o_ref,
                  carry_ref, gh_ref, state_ref, *, n_batch):
    step = pl.program_id(0)
    m_rows = n_batch * CHUNK

    @pl.when(step == 0)
    def _init():
        carry_ref[...] = jnp.zeros_like(carry_ref)
        gh_ref[:, 0:CONV_HIST, :] = jnp.zeros((n_batch, CONV_HIST, D_CONV), F32)
        state_ref[...] = jnp.zeros_like(state_ref)

    vec = lambda row: vec_ref[row:row + 1, :]

    x = x_ref[...].reshape(m_rows, D_MODEL)
    h = _rmsnorm(x, gpre_ref[...]).astype(BF16)
    p = _dot(h, win_ref[...])

    ps = p[:, :D_SHIFT].reshape(n_batch, CHUNK, D_SHIFT)
    prev = pltpu.roll(ps, 1, axis=1)
    frame = lax.broadcasted_iota(jnp.int32, (1, CHUNK, 1), 1)
    prev = jnp.where(frame == 0, carry_ref[...], prev)
    carry_ref[...] = ps[:, CHUNK - 1:CHUNK, :]
    ps = (ps + (prev - ps) * mu_ref[...]).reshape(m_rows, D_SHIFT)

    r = ps[:, 0:D_RWKV]
    k = ps[:, D_RWKV:2 * D_RWKV]
    v = ps[:, 2 * D_RWKV:3 * D_RWKV]
    xl = ps[:, 3 * D_RWKV:D_SHIFT]

    lane = lax.broadcasted_iota(jnp.int32, (1, D_LORA), 1)
    lact = jnp.where(lane < LORA_W, jnp.tanh(xl),
                     jnp.where(lane < LORA_W + LORA_A, xl, jax.nn.sigmoid(xl)))
    lora = _dot(lact.astype(BF16), wlora_ref[...])
    lw = -DECAY_SCALE * jax.nn.sigmoid(vec(V_W0) + lora[:, 0:D_RWKV])
    a = jax.nn.sigmoid(vec(V_A0) + lora[:, D_RWKV:2 * D_RWKV])
    gate = lora[:, 2 * D_RWKV:3 * D_RWKV]

    kk = k * vec(V_KK)
    kk = kk * lax.rsqrt(jnp.maximum(_headsum(kk * kk, e_ref), 1e-12))
    k2 = k * (1.0 + (a - 1.0) * vec(V_KA))
    bonus = _headsum(r * k2 * vec(V_RK), e_ref) * v
    z = -kk
    b = kk * a

    row_t = lax.broadcasted_iota(jnp.int32, (CHUNK, GROUP), 0)
    col_s = lax.broadcasted_iota(jnp.int32, (CHUNK, GROUP), 1) % CHUNK
    strict = col_s < row_t
    incl = col_s <= row_t
    eye = (col_s == row_t).astype(F32)
    lt_r = lax.broadcasted_iota(jnp.int32, (CHUNK, CHUNK), 0)
    lt_c = lax.broadcasted_iota(jnp.int32, (CHUNK, CHUNK), 1)
    lower = (lt_c <= lt_r).astype(BF16)

    half_id = lax.broadcasted_iota(jnp.int32, (CHUNK, LANE_TILE), 1) // HEAD
    in_half = [half_id == 0, half_id == 1]
    half_keep = [m.astype(BF16) for m in in_half]
    tile_of = lambda hd: slice(LANE_TILE * (hd // 2), LANE_TILE * (hd // 2 + 1))
    zero_tile = jnp.zeros((CHUNK, LANE_TILE), BF16)

    def place(tiles):
        return jnp.concatenate(
            [jnp.concatenate([tiles[hd] if j == hd // 2 else zero_tile
                              for j in range(GROUP // LANE_TILE)], axis=1)
             for hd in range(HEADS_PER_GROUP)], axis=0)

    def bd(t):
        tb = t.astype(BF16)
        return place([tb[:, tile_of(hd)] * half_keep[hd % 2] for hd in range(HEADS_PER_GROUP)])

    chains = [(bi, g) for bi in range(n_batch) for g in range(N_GROUPS)]
    stage = lambda fn, *lists: [fn(*args) for args in zip(*lists)]

    def cumulate(bi):
        lw_b = lw[bi * CHUNK:(bi + 1) * CHUNK, :]
        hi, lo = _split2(lw_b)
        return _dot(lower, hi) + _dot(lower, lo)
    cums = [cumulate(bi) for bi in range(n_batch)]

    def prepare(chain):
        bi, g = chain
        rows = slice(bi * CHUNK, (bi + 1) * CHUNK)
        lanes = slice(g * GROUP, (g + 1) * GROUP)
        lw_g = lw[rows, lanes]
        cl = cums[bi][:, lanes]
        ref = cl[MID:MID + 1, :]
        e_out = jnp.exp(ref - cl)
        return dict(
            chain=chain,
            rt=(r[rows, lanes] * jnp.exp(cl - ref)).astype(BF16),
            zt=(z[rows, lanes] * jnp.exp(cl - lw_g - ref)).astype(BF16),
            kt=(k2[rows, lanes] * e_out).astype(BF16),
            bt=(b[rows, lanes] * e_out).astype(BF16),
            decay_in=jnp.exp(ref),
            decay_out=jnp.exp(cl[CHUNK - 1:CHUNK, :] - ref))

    def v_of(ci):
        bi, g = ci["chain"]
        return v[bi * CHUNK:(bi + 1) * CHUNK, g * GROUP:(g + 1) * GROUP]

    def load_state(ci):
        bi, g = ci["chain"]
        st = state_ref[bi, g]
        s0 = [st[HEAD * hd:HEAD * (hd + 1), :] * ci["decay_in"][:, tile_of(hd)]
              for hd in range(HEADS_PER_GROUP)]
        return s0, place([s.astype(BF16) for s in s0])

    def run_wave(wave):
        c = stage(prepare, wave)

        def a_matrices(ci):
            zr = jnp.concatenate([ci["zt"], ci["rt"]], axis=0)
            a_b = _dot_nt(zr, bd(ci["bt"]))
            a_k = _dot_nt(zr, bd(ci["kt"]))
            return dict(ci, a_bz=jnp.where(strict, a_b[:CHUNK], 0.0),
                        a_br=jnp.where(incl, a_b[CHUNK:], 0.0).astype(BF16),
                        a_kz=jnp.where(strict, a_k[:CHUNK], 0.0).astype(BF16),
                        a_kr=jnp.where(incl, a_k[CHUNK:], 0.0).astype(BF16))
        c = stage(a_matrices, c)

        t_acc = stage(lambda ci: eye + ci["a_bz"], c)
        pw = stage(lambda ci: _dot(ci["a_bz"].astype(BF16), bd(ci["a_bz"])), c)
        n = 2
        while n < CHUNK // 2:
            res = stage(lambda t, q: _dot(jnp.concatenate([t, q], axis=0).astype(BF16), bd(q)), t_acc, pw)
            t_acc = stage(lambda t, rr: t + rr[:CHUNK], t_acc, res)
            pw = stage(lambda rr: rr[CHUNK:], res)
            n *= 2
        t_acc = stage(lambda t, q: t + _dot(t.astype(BF16), bd(q)), t_acc, pw)

        s0 = stage(load_state, c)
        vbd = stage(lambda ci: bd(v_of(ci)), c)
        rhs0 = stage(lambda ci, s, vb: _dot_nt(ci["zt"], s[1]) + _dot(ci["a_kz"], vb), c, s0, vbd)
        u = stage(lambda t, q: _dot(t.astype(BF16), bd(q)), t_acc, rhs0)
        y = stage(lambda ci, s, ui, vb: _dot_nt(ci["rt"], s[1]) + _dot(ci["a_br"], bd(ui))
                  + _dot(ci["a_kr"], vb), c, s0, u, vbd)
        upd = stage(lambda ci, ui: _dot_tn(jnp.concatenate([ui, v_of(ci)], axis=0).astype(BF16),
                                           jnp.concatenate([ci["bt"], ci["kt"]], axis=0)), c, u)
        for ci, s, di in zip(c, s0, upd):
            bi, g = ci["chain"]
            state_ref[bi, g] = jnp.concatenate(
                [(s[0][hd] + jnp.where(in_half[hd % 2], di[HEAD * hd:HEAD * (hd + 1), tile_of(hd)], 0.0))
                 * ci["decay_out"][:, tile_of(hd)] for hd in range(HEADS_PER_GROUP)], axis=0)
        return y

    y = []
    for first in range(0, len(chains), WAVE):
        y += run_wave(chains[first:first + WAVE])

    y = jnp.concatenate([jnp.concatenate(y[N_GROUPS * bi:N_GROUPS * (bi + 1)], axis=1)
                         for bi in range(n_batch)], axis=0)
    mean = _headsum(y, e_ref, split=True) * (1.0 / HEAD)
    yc = y - mean
    var = _headsum(yc * yc, e_ref) * (1.0 / HEAD)
    yn = yc * lax.rsqrt(var + GN_EPS) * vec(V_GNW) + vec(V_GNB)
    out_a = (yn + bonus) * gate

    pc = p[:, D_SHIFT:]
    glu = pc[:, :D_CONV] * jax.nn.sigmoid(pc[:, D_CONV:])
    gh_ref[:, CONV_HIST:CONV_HIST + CHUNK, :] = glu.reshape(n_batch, CHUNK, D_CONV)
    conv_rows = []
    for bi in range(n_batch):
        gh = gh_ref[bi]
        acc = jnp.zeros((CHUNK, D_CONV), F32) + vec(V_CB)
        for rr in range(SUBLANES):
            shifted = pltpu.roll(gh, rr, axis=0) if rr else gh
            for q in range(CONV_HIST // SUBLANES):
                s = SUBLANES * q + rr
                if s >= CONV_WIDTH:
                    continue
                w = CONV_WIDTH - 1 - s
                lo = CONV_HIST - SUBLANES * q
                acc = acc + shifted[lo:lo + CHUNK, :] * dw_ref[w:w + 1, :]
        conv_rows.append(acc)
    gh_ref[:, 0:CONV_HIST, :] = gh_ref[:, CHUNK:CHUNK + CONV_HIST, :]
    cv = jnp.concatenate(conv_rows, axis=0)
    c_mean = jnp.mean(cv, axis=-1, keepdims=True)
    cc = cv - c_mean
    c_var = jnp.mean(cc * cc, axis=-1, keepdims=True)
    out_b = jax.nn.silu(cc * lax.rsqrt(c_var + LN_EPS) * vec(V_LNW) + vec(V_LNB))

    m = (_dot(out_a.astype(BF16), wout_ref[0:D_RWKV, :])
         + _dot(out_b.astype(BF16), wout_ref[D_RWKV:, :]))
    o_ref[...] = (x + _rmsnorm(m, gpost_ref[...])).reshape(n_batch, CHUNK, D_MODEL)


def _mixer(x, g_pre, g_post, w_in, mu, w_lora, vec, dw, e_mat, w_out):
    n_batch, seq, _ = x.shape
    const = lambda i: (0, 0)
    one = dict(pipeline_mode=pl.Buffered(1))
    return pl.pallas_call(
        functools.partial(_mixer_kernel, n_batch=n_batch),
        out_shape=jax.ShapeDtypeStruct(x.shape, F32),
        grid=(seq // CHUNK,),
        in_specs=[
            pl.BlockSpec((n_batch, CHUNK, D_MODEL), lambda i: (0, i, 0)),
            pl.BlockSpec((1, D_MODEL), const),
            pl.BlockSpec((1, D_MODEL), const),
            pl.BlockSpec((D_MODEL, D_IN), const, **one),
            pl.BlockSpec((1, D_SHIFT), const),
            pl.BlockSpec((D_LORA, 3 * D_RWKV), const, **one),
            pl.BlockSpec((N_VEC, D_RWKV), const),
            pl.BlockSpec((CONV_HIST, D_CONV), const),
            pl.BlockSpec((GROUP, GROUP), const, **one),
            pl.BlockSpec((D_MODEL, D_MODEL), const, **one),
        ],
        out_specs=pl.BlockSpec((n_batch, CHUNK, D_MODEL), lambda i: (0, i, 0)),
        scratch_shapes=[
            pltpu.VMEM((n_batch, 1, D_SHIFT), F32),
            pltpu.VMEM((n_batch, CONV_HIST + CHUNK, D_CONV), F32),
            pltpu.VMEM((n_batch, N_GROUPS, GROUP, LANE_TILE), F32),
        ],
        compiler_params=pltpu.CompilerParams(
            dimension_semantics=("arbitrary",), vmem_limit_bytes=VMEM_LIMIT),
        name="mixer",
    )(x, g_pre, g_post, w_in, mu, w_lora, vec, dw, e_mat, w_out)


def kernel(x, ffn1_norm_pre, ffn1_norm_post, ffn1_w_gu, ffn1_w_down, mix_norm_pre, mix_norm_post, w_in, shift_mu, w_up, w0, a_up, a0, g_up, k_k, k_a, r_k, gn_w, gn_b, conv_dw, conv_b, conv_ln_w, conv_ln_b, w_out, ffn2_norm_pre, ffn2_norm_post, ffn2_w_gu, ffn2_w_down):
    n_batch, seq, d_model = x.shape
    depth = ffn1_w_gu.shape[0]
    assert d_model == D_MODEL and seq % CHUNK == 0 and (n_batch * seq) % FFN_TM == 0

    head_id = jnp.arange(GROUP) // HEAD
    e_mat = (head_id[:, None] == head_id[None, :]).astype(BF16)

    for l in range(depth):
        w_lora = jnp.zeros((D_LORA, 3 * D_RWKV), F32)
        w_lora = w_lora.at[0:LORA_W, 0:D_RWKV].set(w_up[l])
        w_lora = w_lora.at[LORA_W:LORA_W + LORA_A, D_RWKV:2 * D_RWKV].set(a_up[l])
        w_lora = w_lora.at[LORA_W + LORA_A:, 2 * D_RWKV:].set(g_up[l])
        rows = [w0[l], a0[l], k_k[l], k_a[l], r_k[l].reshape(D_RWKV), gn_w[l], gn_b[l],
                conv_b[l], conv_ln_w[l], conv_ln_b[l]]
        vec = jnp.zeros((N_VEC, D_RWKV), F32).at[0:len(rows)].set(jnp.stack(rows))
        dw = jnp.zeros((CONV_HIST, D_CONV), F32).at[0:CONV_WIDTH].set(conv_dw[l])

        x2d = _ffn(x.reshape(n_batch * seq, D_MODEL),
                   ffn1_norm_pre[l].reshape(1, D_MODEL), ffn1_norm_post[l].reshape(1, D_MODEL),
                   ffn1_w_gu[l].astype(BF16), ffn1_w_down[l].astype(BF16))
        x = _mixer(x2d.reshape(n_batch, seq, D_MODEL),
                   mix_norm_pre[l].reshape(1, D_MODEL), mix_norm_post[l].reshape(1, D_MODEL),
                   w_in[l].astype(BF16), shift_mu[l].reshape(1, D_SHIFT), w_lora.astype(BF16),
                   vec, dw, e_mat, w_out[l].astype(BF16))
        x2d = _ffn(x.reshape(n_batch * seq, D_MODEL),
                   ffn2_norm_pre[l].reshape(1, D_MODEL), ffn2_norm_post[l].reshape(1, D_MODEL),
                   ffn2_w_gu[l].astype(BF16), ffn2_w_down[l].astype(BF16))
        x = x2d.reshape(n_batch, seq, D_MODEL)
    return x
```

```python
import functools
import math

import jax
import jax.numpy as jnp
from jax import lax
from jax.experimental import pallas as pl
from jax.experimental.pallas import tpu as pltpu

F32 = jnp.float32
BF16 = jnp.bfloat16

D_MODEL = 1024
D_FF = 2816
D_RWKV = 512
D_CONV = 512
HEAD = 64
LORA_W = 64
LORA_A = 64
LORA_G = 128
D_LORA = LORA_W + LORA_A + LORA_G
D_SHIFT = 3 * D_RWKV + D_LORA
D_IN = D_SHIFT + 2 * D_CONV
CONV_WIDTH = 31
RMS_EPS = 1e-6
GN_EPS = 64e-5
LN_EPS = 1e-5
DECAY_SCALE = math.exp(-0.5)

CHUNK = 64
GROUP = 256
N_GROUPS = D_RWKV // GROUP
HEADS_PER_GROUP = GROUP // HEAD
MID = CHUNK // 2 - 1
CONV_HIST = 32
LANE_TILE = 128
SUBLANES = 8
WAVE = 8
assert CHUNK == HEAD

FFN_TM = 1024
FFN_FC = 256
FFN_CHUNKS = D_FF // FFN_FC
FFN_DOWN_ROWS = 256
VMEM_LIMIT = 56 * 1024 * 1024

(V_W0, V_A0, V_KK, V_KA, V_RK, V_GNW, V_GNB, V_CB, V_LNW, V_LNB) = range(10)
N_VEC = 16


def _rmsnorm(x, g):
    ms = jnp.mean(x * x, axis=-1, keepdims=True)
    return x * lax.rsqrt(ms + RMS_EPS) * g


def _dot(a, b):
    return jnp.dot(a, b, preferred_element_type=F32)


def _dot_nt(a, b):
    return lax.dot_general(a, b, (((1,), (1,)), ((), ())), preferred_element_type=F32)


def _dot_tn(a, b):
    return lax.dot_general(a, b, (((0,), (0,)), ((), ())), preferred_element_type=F32)


def _ffn_kernel(x_ref, gpre_ref, gpost_ref, wg_ref, wu_ref, wd_ref, o_ref, act_ref, wg_s, wu_s, wd_s):
    step = pl.program_id(0)

    @pl.when(step < FFN_CHUNKS)
    def _stage_weights():
        wg_s[step] = wg_ref[...].astype(BF16)
        wu_s[step] = wu_ref[...].astype(BF16)
        wd_s[pl.ds(pl.multiple_of(step * FFN_FC, FFN_FC), FFN_FC), :] = wd_ref[...].astype(BF16)

    @pl.when(step >= FFN_CHUNKS)
    def _tile():
        x = x_ref[...]
        inv = lax.rsqrt(jnp.mean(x * x, axis=-1, keepdims=True) + RMS_EPS)
        h = (x * gpre_ref[...]).astype(BF16)
        for c in range(FFN_CHUNKS):
            lo = c * FFN_FC
            gate = _dot(h, wg_s[c]) * inv
            up = _dot(h, wu_s[c]) * inv
            act_ref[:, lo:lo + FFN_FC] = (jax.nn.silu(gate) * up).astype(BF16)
        for blk in range(FFN_TM // FFN_DOWN_ROWS):
            rows = slice(blk * FFN_DOWN_ROWS, (blk + 1) * FFN_DOWN_ROWS)
            f = _dot(act_ref[rows, :], wd_s[...])
            o_ref[rows, :] = x_ref[rows, :] + 0.5 * _rmsnorm(f, gpost_ref[...])


def _ffn(x2d, g_pre, g_post, w_gu, w_down):
    n_tok = x2d.shape[0]
    const = lambda i: (0, 0)
    tile = lambda i: (jnp.maximum(i - FFN_CHUNKS, 0), 0)
    chunk = lambda i: jnp.minimum(i, FFN_CHUNKS - 1)
    return pl.pallas_call(
        _ffn_kernel,
        out_shape=jax.ShapeDtypeStruct((n_tok, D_MODEL), F32),
        grid=(FFN_CHUNKS + n_tok // FFN_TM,),
        in_specs=[
            pl.BlockSpec((FFN_TM, D_MODEL), tile),
            pl.BlockSpec((1, D_MODEL), const),
            pl.BlockSpec((1, D_MODEL), const),
            pl.BlockSpec((D_MODEL, FFN_FC), lambda i: (0, chunk(i))),
            pl.BlockSpec((D_MODEL, FFN_FC), lambda i: (0, FFN_CHUNKS + chunk(i))),
            pl.BlockSpec((FFN_FC, D_MODEL), lambda i: (chunk(i), 0)),
        ],
        out_specs=pl.BlockSpec((FFN_TM, D_MODEL), tile),
        scratch_shapes=[
            pltpu.VMEM((FFN_TM, D_FF), BF16),
            pltpu.VMEM((FFN_CHUNKS, D_MODEL, FFN_FC), BF16),
            pltpu.VMEM((FFN_CHUNKS, D_MODEL, FFN_FC), BF16),
            pltpu.VMEM((D_FF, D_MODEL), BF16),
        ],
        compiler_params=pltpu.CompilerParams(
            dimension_semantics=("arbitrary",), vmem_limit_bytes=VMEM_LIMIT),
        name="ffn",
    )(x2d, g_pre, g_post, w_gu, w_gu, w_down)


def _split2(x):
    hi = x.astype(BF16)
    lo = (x - hi.astype(F32)).astype(BF16)
    return hi, lo


def _headsum(x, e_ref, split=False):
    e = e_ref[...]
    hi, lo = _split2(x) if split else (x.astype(BF16), None)
    out = []
    for g in range(N_GROUPS):
        lanes = slice(g * GROUP, (g + 1) * GROUP)
        s = _dot(hi[:, lanes], e)
        if split:
            s = s + _dot(lo[:, lanes], e)
        out.append(s)
    return jnp.concatenate(out, axis=-1)


def _mixer_kernel(x_ref, gpre_ref, gpost_ref, win_ref, mu_ref, wlora_ref, vec_ref,
                  dw_ref, e_ref, wout_ref, o_ref,
                  carry_ref, gh_ref, state_ref, *, n_batch):
    step = pl.program_id(0)
    m_rows = n_batch * CHUNK

    @pl.when(step == 0)
    def _init():
        carry_ref[...] = jnp.zeros_like(carry_ref)
        gh_ref[:, 0:CONV_HIST, :] = jnp.zeros((n_batch, CONV_HIST, D_CONV), F32)
        state_ref[...] = jnp.zeros_like(state_ref)

    vec = lambda row: vec_ref[row:row + 1, :]

    x = x_ref[...].reshape(m_rows, D_MODEL)
    h = _rmsnorm(x, gpre_ref[...]).astype(BF16)
    p = _dot(h, win_ref[...])

    ps = p[:, :D_SHIFT].reshape(n_batch, CHUNK, D_SHIFT)
    prev = pltpu.roll(ps, 1, axis=1)
    frame = lax.broadcasted_iota(jnp.int32, (1, CHUNK, 1), 1)
    prev = jnp.where(frame == 0, carry_ref[...], prev)
    carry_ref[...] = ps[:, CHUNK - 1:CHUNK, :]
    ps = (ps + (prev - ps) * mu_ref[...]).reshape(m_rows, D_SHIFT)

    r = ps[:, 0:D_RWKV]
    k = ps[:, D_RWKV:2 * D_RWKV]
    v = ps[:, 2 * D_RWKV:3 * D_RWKV]
    xl = ps[:, 3 * D_RWKV:D_SHIFT]

    lane = lax.broadcasted_iota(jnp.int32, (1, D_LORA), 1)
    lact = jnp.where(lane < LORA_W, jnp.tanh(xl),
                     jnp.where(lane < LORA_W + LORA_A, xl, jax.nn.sigmoid(xl)))
    lora = _dot(lact.astype(BF16), wlora_ref[...])
    lw = -DECAY_SCALE * jax.nn.sigmoid(vec(V_W0) + lora[:, 0:D_RWKV])
    a = jax.nn.sigmoid(vec(V_A0) + lora[:, D_RWKV:2 * D_RWKV])
    gate = lora[:, 2 * D_RWKV:3 * D_RWKV]

    kk = k * vec(V_KK)
    kk = kk * lax.rsqrt(jnp.maximum(_headsum(kk * kk, e_ref), 1e-12))
    k2 = k * (1.0 + (a - 1.0) * vec(V_KA))
    bonus = _headsum(r * k2 * vec(V_RK), e_ref) * v
    z = -kk
    b = kk * a

    row_t = lax.broadcasted_iota(jnp.int32, (CHUNK, GROUP), 0)
    col_s = lax.broadcasted_iota(jnp.int32, (CHUNK, GROUP), 1) % CHUNK
    strict = col_s < row_t
    incl = col_s <= row_t
    eye = (col_s == row_t).astype(F32)
    lt_r = lax.broadcasted_iota(jnp.int32, (CHUNK, CHUNK), 0)
    lt_c = lax.broadcasted_iota(jnp.int32, (CHUNK, CHUNK), 1)
    lower = (lt_c <= lt_r).astype(BF16)

    half_id = lax.broadcasted_iota(jnp.int32, (CHUNK, LANE_TILE), 1) // HEAD
    in_half = [half_id == 0, half_id == 1]
    half_keep = [m.astype(BF16) for m in in_half]
    tile_of = lambda hd: slice(LANE_TILE * (hd // 2), LANE_TILE * (hd // 2 + 1))
    zero_tile = jnp.zeros((CHUNK, LANE_TILE), BF16)

    def place(tiles):
        return jnp.concatenate(
            [jnp.concatenate([tiles[hd] if j == hd // 2 else zero_tile
                              for j in range(GROUP // LANE_TILE)], axis=1)
             for hd in range(HEADS_PER_GROUP)], axis=0)

    def bd(t):
        tb = t.astype(BF16)
        return place([tb[:, tile_of(hd)] * half_keep[hd % 2] for hd in range(HEADS_PER_GROUP)])

    chains = [(bi, g) for bi in range(n_batch) for g in range(N_GROUPS)]
    stage = lambda fn, *lists: [fn(*args) for args in zip(*lists)]

    def cumulate(bi):
        lw_b = lw[bi * CHUNK:(bi + 1) * CHUNK, :]
        hi, lo = _split2(lw_b)
        return _dot(lower, hi) + _dot(lower, lo)
    cums = [cumulate(bi) for bi in range(n_batch)]

    def prepare(chain):
        bi, g = chain
        rows = slice(bi * CHUNK, (bi + 1) * CHUNK)
        lanes = slice(g * GROUP, (g + 1) * GROUP)
        lw_g = lw[rows, lanes]
        cl = cums[bi][:, lanes]
        ref = cl[MID:MID + 1, :]
        e_out = jnp.exp(ref - cl)
        return dict(
            chain=chain,
            rt=(r[rows, lanes] * jnp.exp(cl - ref)).astype(BF16),
            zt=(z[rows, lanes] * jnp.exp(cl - lw_g - ref)).astype(BF16),
            kt=(k2[rows, lanes] * e_out).astype(BF16),
            bt=(b[rows, lanes] * e_out).astype(BF16),
            decay_in=jnp.exp(ref),
            decay_out=jnp.exp(cl[CHUNK - 1:CHUNK, :] - ref))

    def v_of(ci):
        bi, g = ci["chain"]
        return v[bi * CHUNK:(bi + 1) * CHUNK, g * GROUP:(g + 1) * GROUP]

    def load_state(ci):
        bi, g = ci["chain"]
        st = state_ref[bi, g]
        s0 = [st[HEAD * hd:HEAD * (hd + 1), :] * ci["decay_in"][:, tile_of(hd)]
              for hd in range(HEADS_PER_GROUP)]
        return s0, place([s.astype(BF16) for s in s0])

    def run_wave(wave):
        c = stage(prepare, wave)

        def a_matrices(ci):
            zr = jnp.concatenate([ci["zt"], ci["rt"]], axis=0)
            a_b = _dot_nt(zr, bd(ci["bt"]))
            a_k = _dot_nt(zr, bd(ci["kt"]))
            return dict(ci, a_bz=jnp.where(strict, a_b[:CHUNK], 0.0),
                        a_br=jnp.where(incl, a_b[CHUNK:], 0.0).astype(BF16),
                        a_kz=jnp.where(strict, a_k[:CHUNK], 0.0).astype(BF16),
                        a_kr=jnp.where(incl, a_k[CHUNK:], 0.0).astype(BF16))
        c = stage(a_matrices, c)

        t_acc = stage(lambda ci: eye + ci["a_bz"], c)
        pw = stage(lambda ci: _dot(ci["a_bz"].astype(BF16), bd(ci["a_bz"])), c)
        n = 2
        while n < CHUNK // 2:
            res = stage(lambda t, q: _dot(jnp.concatenate([t, q], axis=0).astype(BF16), bd(q)), t_acc, pw)
            t_acc = stage(lambda t, rr: t + rr[:CHUNK], t_acc, res)
            pw = stage(lambda rr: rr[CHUNK:], res)
            n *= 2
        t_acc = stage(lambda t, q: t + _dot(t.astype(BF16), bd(q)), t_acc, pw)

        s0 = stage(load_state, c)
        vbd = stage(lambda ci: bd(v_of(ci)), c)
        rhs0 = stage(lambda ci, s, vb: _dot_nt(ci["zt"], s[1]) + _dot(ci["a_kz"], vb), c, s0, vbd)
        u = stage(lambda t, q: _dot(t.astype(BF16), bd(q)), t_acc, rhs0)
        y = stage(lambda ci, s, ui, vb: _dot_nt(ci["rt"], s[1]) + _dot(ci["a_br"], bd(ui))
                  + _dot(ci["a_kr"], vb), c, s0, u, vbd)
        upd = stage(lambda ci, ui: _dot_tn(jnp.concatenate([ui, v_of(ci)], axis=0).astype(BF16),
                                           jnp.concatenate([ci["bt"], ci["kt"]], axis=0)), c, u)
        for ci, s, di in zip(c, s0, upd):
            bi, g = ci["chain"]
            state_ref[bi, g] = jnp.concatenate(
                [(s[0][hd] + jnp.where(in_half[hd % 2], di[HEAD * hd:HEAD * (hd + 1), tile_of(hd)], 0.0))
                 * ci["decay_out"][:, tile_of(hd)] for hd in range(HEADS_PER_GROUP)], axis=0)
        return y

    y = []
    for first in range(0, len(chains), WAVE):
        y += run_wave(chains[first:first + WAVE])

    y = jnp.concatenate([jnp.concatenate(y[N_GROUPS * bi:N_GROUPS * (bi + 1)], axis=1)
                         for bi in range(n_batch)], axis=0)
    mean = _headsum(y, e_ref, split=True) * (1.0 / HEAD)
    yc = y - mean
    var = _headsum(yc * yc, e_ref) * (1.0 / HEAD)
    yn = yc * lax.rsqrt(var + GN_EPS) * vec(V_GNW) + vec(V_GNB)
    out_a = (yn + bonus) * gate

    pc = p[:, D_SHIFT:]
    glu = pc[:, :D_CONV] * jax.nn.sigmoid(pc[:, D_CONV:])
    gh_ref[:, CONV_HIST:CONV_HIST + CHUNK, :] = glu.reshape(n_batch, CHUNK, D_CONV)
    conv_rows = []
    for bi in range(n_batch):
        gh = gh_ref[bi]
        acc = jnp.zeros((CHUNK, D_CONV), F32) + vec(V_CB)
        for rr in range(SUBLANES):
            shifted = pltpu.roll(gh, rr, axis=0) if rr else gh
            for q in range(CONV_HIST // SUBLANES):
                s = SUBLANES * q + rr
                if s >= CONV_WIDTH:
                    continue
                w = CONV_WIDTH - 1 - s
                lo = CONV_HIST - SUBLANES * q
                acc = acc + shifted[lo:lo + CHUNK, :] * dw_ref[w:w + 1, :]
        conv_rows.append(acc)
    gh_ref[:, 0:CONV_HIST, :] = gh_ref[:, CHUNK:CHUNK + CONV_HIST, :]
    cv = jnp.concatenate(conv_rows, axis=0)
    c_mean = jnp.mean(cv, axis=-1, keepdims=True)
    cc = cv - c_mean
    c_var = jnp.mean(cc * cc, axis=-1, keepdims=True)
    out_b = jax.nn.silu(cc * lax.rsqrt(c_var + LN_EPS) * vec(V_LNW) + vec(V_LNB))

    m = (_dot(out_a.astype(BF16), wout_ref[0:D_RWKV, :])
         + _dot(out_b.astype(BF16), wout_ref[D_RWKV:, :]))
    o_ref[...] = (x + _rmsnorm(m, gpost_ref[...])).reshape(n_batch, CHUNK, D_MODEL)


def _mixer(x, g_pre, g_post, w_in, mu, w_lora, vec, dw, e_mat, w_out):
    n_batch, seq, _ = x.shape
    const = lambda i: (0, 0)
    one = dict(pipeline_mode=pl.Buffered(1))
    return pl.pallas_call(
        functools.partial(_mixer_kernel, n_batch=n_batch),
        out_shape=jax.ShapeDtypeStruct(x.shape, F32),
        grid=(seq // CHUNK,),
        in_specs=[
            pl.BlockSpec((n_batch, CHUNK, D_MODEL), lambda i: (0, i, 0)),
            pl.BlockSpec((1, D_MODEL), const),
            pl.BlockSpec((1, D_MODEL), const),
            pl.BlockSpec((D_MODEL, D_IN), const, **one),
            pl.BlockSpec((1, D_SHIFT), const),
            pl.BlockSpec((D_LORA, 3 * D_RWKV), const, **one),
            pl.BlockSpec((N_VEC, D_RWKV), const),
            pl.BlockSpec((CONV_HIST, D_CONV), const),
            pl.BlockSpec((GROUP, GROUP), const, **one),
            pl.BlockSpec((D_MODEL, D_MODEL), const, **one),
        ],
        out_specs=pl.BlockSpec((n_batch, CHUNK, D_MODEL), lambda i: (0, i, 0)),
        scratch_shapes=[
            pltpu.VMEM((n_batch, 1, D_SHIFT), F32),
            pltpu.VMEM((n_batch, CONV_HIST + CHUNK, D_CONV), F32),
            pltpu.VMEM((n_batch, N_GROUPS, GROUP, LANE_TILE), F32),
        ],
        compiler_params=pltpu.CompilerParams(
            dimension_semantics=("arbitrary",), vmem_limit_bytes=VMEM_LIMIT),
        name="mixer",
    )(x, g_pre, g_post, w_in, mu, w_lora, vec, dw, e_mat, w_out)


def kernel(x, ffn1_norm_pre, ffn1_norm_post, ffn1_w_gu, ffn1_w_down, mix_norm_pre, mix_norm_post, w_in, shift_mu, w_up, w0, a_up, a0, g_up, k_k, k_a, r_k, gn_w, gn_b, conv_dw, conv_b, conv_ln_w, conv_ln_b, w_out, ffn2_norm_pre, ffn2_norm_post, ffn2_w_gu, ffn2_w_down):
    n_batch, seq, d_model = x.shape
    depth = ffn1_w_gu.shape[0]
    assert d_model == D_MODEL and seq % CHUNK == 0 and (n_batch * seq) % FFN_TM == 0

    head_id = jnp.arange(GROUP) // HEAD
    e_mat = (head_id[:, None] == head_id[None, :]).astype(BF16)

    for l in range(depth):
        w_lora = jnp.zeros((D_LORA, 3 * D_RWKV), F32)
        w_lora = w_lora.at[0:LORA_W, 0:D_RWKV].set(w_up[l])
        w_lora = w_lora.at[LORA_W:LORA_W + LORA_A, D_RWKV:2 * D_RWKV].set(a_up[l])
        w_lora = w_lora.at[LORA_W + LORA_A:, 2 * D_RWKV:].set(g_up[l])
        rows = [w0[l], a0[l], k_k[l], k_a[l], r_k[l].reshape(D_RWKV), gn_w[l], gn_b[l],
                conv_b[l], conv_ln_w[l], conv_ln_b[l]]
        vec = jnp.zeros((N_VEC, D_RWKV), F32).at[0:len(rows)].set(jnp.stack(rows))
        dw = jnp.zeros((CONV_HIST, D_CONV), F32).at[0:CONV_WIDTH].set(conv_dw[l])

        x2d = _ffn(x.reshape(n_batch * seq, D_MODEL),
                   ffn1_norm_pre[l].reshape(1, D_MODEL), ffn1_norm_post[l].reshape(1, D_MODEL),
                   ffn1_w_gu[l], ffn1_w_down[l])
        x = _mixer(x2d.reshape(n_batch, seq, D_MODEL),
                   mix_norm_pre[l].reshape(1, D_MODEL), mix_norm_post[l].reshape(1, D_MODEL),
                   w_in[l].astype(BF16), shift_mu[l].reshape(1, D_SHIFT), w_lora.astype(BF16),
                   vec, dw, e_mat, w_out[l].astype(BF16))
        x2d = _ffn(x.reshape(n_batch * seq, D_MODEL),
                   ffn2_norm_pre[l].reshape(1, D_MODEL), ffn2_norm_post[l].reshape(1, D_MODEL),
                   ffn2_w_gu[l], ffn2_w_down[l])
        x = x2d.reshape(n_batch, seq, D_MODEL)
    return x
```

```python
import functools
import math

import jax
import jax.numpy as jnp
from jax import lax
from jax.experimental import pallas as pl
from jax.experimental.pallas import tpu as pltpu

F32 = jnp.float32
BF16 = jnp.bfloat16

D_MODEL = 1024
D_FF = 2816
D_RWKV = 512
D_CONV = 512
HEAD = 64
LORA_W = 64
LORA_A = 64
LORA_G = 128
D_LORA = LORA_W + LORA_A + LORA_G
D_SHIFT = 3 * D_RWKV + D_LORA
D_IN = D_SHIFT + 2 * D_CONV
CONV_WIDTH = 31
RMS_EPS = 1e-6
GN_EPS = 64e-5
LN_EPS = 1e-5
DECAY_SCALE = math.exp(-0.5)

CHUNK = 64
GROUP = 256
N_GROUPS = D_RWKV // GROUP
HEADS_PER_GROUP = GROUP // HEAD
MID = CHUNK // 2 - 1
CONV_HIST = 32
LANE_TILE = 128
SUBLANES = 8
WAVE = 8
MIX_STAGE = 4
MIX_ROWS = D_MODEL // MIX_STAGE
assert CHUNK == HEAD

FFN_TM = 1024
FFN_FC = 256
FFN_CHUNKS = D_FF // FFN_FC
FFN_DOWN_ROWS = 256
VMEM_LIMIT = 56 * 1024 * 1024

(V_W0, V_A0, V_KK, V_KA, V_RK, V_GNW, V_GNB, V_CB, V_LNW, V_LNB) = range(10)
N_VEC = 16


def _rmsnorm(x, g):
    ms = jnp.mean(x * x, axis=-1, keepdims=True)
    return x * lax.rsqrt(ms + RMS_EPS) * g


def _dot(a, b):
    return jnp.dot(a, b, preferred_element_type=F32)


def _dot_nt(a, b):
    return lax.dot_general(a, b, (((1,), (1,)), ((), ())), preferred_element_type=F32)


def _dot_tn(a, b):
    return lax.dot_general(a, b, (((0,), (0,)), ((), ())), preferred_element_type=F32)


def _ffn_kernel(x_ref, gpre_ref, gpost_ref, wg_ref, wu_ref, wd_ref, o_ref, act_ref, wg_s, wu_s, wd_s):
    step = pl.program_id(0)

    @pl.when(step < FFN_CHUNKS)
    def _stage_weights():
        wg_s[step] = wg_ref[...].astype(BF16)
        wu_s[step] = wu_ref[...].astype(BF16)
        wd_s[pl.ds(pl.multiple_of(step * FFN_FC, FFN_FC), FFN_FC), :] = wd_ref[...].astype(BF16)

    @pl.when(step >= FFN_CHUNKS)
    def _tile():
        x = x_ref[...]
        inv = lax.rsqrt(jnp.mean(x * x, axis=-1, keepdims=True) + RMS_EPS)
        h = (x * gpre_ref[...]).astype(BF16)
        for c in range(FFN_CHUNKS):
            lo = c * FFN_FC
            gate = _dot(h, wg_s[c]) * inv
            up = _dot(h, wu_s[c]) * inv
            act_ref[:, lo:lo + FFN_FC] = (jax.nn.silu(gate) * up).astype(BF16)
        for blk in range(FFN_TM // FFN_DOWN_ROWS):
            rows = slice(blk * FFN_DOWN_ROWS, (blk + 1) * FFN_DOWN_ROWS)
            f = _dot(act_ref[rows, :], wd_s[...])
            o_ref[rows, :] = x_ref[rows, :] + 0.5 * _rmsnorm(f, gpost_ref[...])


def _ffn(x2d, g_pre, g_post, w_gu, w_down):
    n_tok = x2d.shape[0]
    const = lambda i: (0, 0)
    tile = lambda i: (jnp.maximum(i - FFN_CHUNKS, 0), 0)
    chunk = lambda i: jnp.minimum(i, FFN_CHUNKS - 1)
    return pl.pallas_call(
        _ffn_kernel,
        out_shape=jax.ShapeDtypeStruct((n_tok, D_MODEL), F32),
        grid=(FFN_CHUNKS + n_tok // FFN_TM,),
        in_specs=[
            pl.BlockSpec((FFN_TM, D_MODEL), tile),
            pl.BlockSpec((1, D_MODEL), const),
            pl.BlockSpec((1, D_MODEL), const),
            pl.BlockSpec((D_MODEL, FFN_FC), lambda i: (0, chunk(i))),
            pl.BlockSpec((D_MODEL, FFN_FC), lambda i: (0, FFN_CHUNKS + chunk(i))),
            pl.BlockSpec((FFN_FC, D_MODEL), lambda i: (chunk(i), 0)),
        ],
        out_specs=pl.BlockSpec((FFN_TM, D_MODEL), tile),
        scratch_shapes=[
            pltpu.VMEM((FFN_TM, D_FF), BF16),
            pltpu.VMEM((FFN_CHUNKS, D_MODEL, FFN_FC), BF16),
            pltpu.VMEM((FFN_CHUNKS, D_MODEL, FFN_FC), BF16),
            pltpu.VMEM((D_FF, D_MODEL), BF16),
        ],
        compiler_params=pltpu.CompilerParams(
            dimension_semantics=("arbitrary",), vmem_limit_bytes=VMEM_LIMIT),
        name="ffn",
    )(x2d, g_pre, g_post, w_gu, w_gu, w_down)


def _split2(x):
    hi = x.astype(BF16)
    lo = (x - hi.astype(F32)).astype(BF16)
    return hi, lo


def _headsum(x, e_ref, split=False):
    e = e_ref[...]
    hi, lo = _split2(x) if split else (x.astype(BF16), None)
    out = []
    for g in range(N_GROUPS):
        lanes = slice(g * GROUP, (g + 1) * GROUP)
        s = _dot(hi[:, lanes], e)
        if split:
            s = s + _dot(lo[:, lanes], e)
        out.append(s)
    return jnp.concatenate(out, axis=-1)


def _mixer_kernel(x_ref, gpre_ref, gpost_ref, win_f32_ref, mu_ref, wlora_ref, vec_ref,
                  dw_ref, e_ref, wout_f32_ref, o_ref,
                  carry_ref, gh_ref, state_ref, win_s, wout_s, *, n_batch):
    pid = pl.program_id(0)

    @pl.when(pid < MIX_STAGE)
    def _stage_weights():
        rows = pl.ds(pl.multiple_of(pid * MIX_ROWS, MIX_ROWS), MIX_ROWS)
        win_s[rows, :] = win_f32_ref[...].astype(BF16)
        wout_s[rows, :] = wout_f32_ref[...].astype(BF16)

    @pl.when(pid >= MIX_STAGE)
    def _tile():
        _mixer_tile(x_ref, gpre_ref, gpost_ref, win_s, mu_ref, wlora_ref, vec_ref,
                    dw_ref, e_ref, wout_s, o_ref, carry_ref, gh_ref, state_ref, n_batch=n_batch)


def _mixer_tile(x_ref, gpre_ref, gpost_ref, win_ref, mu_ref, wlora_ref, vec_ref,
                dw_ref, e_ref, wout_ref, o_ref,
                carry_ref, gh_ref, state_ref, *, n_batch):
    step = pl.program_id(0) - MIX_STAGE
    m_rows = n_batch * CHUNK

    @pl.when(step == 0)
    def _init():
        carry_ref[...] = jnp.zeros_like(carry_ref)
        gh_ref[:, 0:CONV_HIST, :] = jnp.zeros((n_batch, CONV_HIST, D_CONV), F32)
        state_ref[...] = jnp.zeros_like(state_ref)

    vec = lambda row: vec_ref[row:row + 1, :]

    x = x_ref[...].reshape(m_rows, D_MODEL)
    h = _rmsnorm(x, gpre_ref[...]).astype(BF16)
    p = _dot(h, win_ref[...])

    ps = p[:, :D_SHIFT].reshape(n_batch, CHUNK, D_SHIFT)
    prev = pltpu.roll(ps, 1, axis=1)
    frame = lax.broadcasted_iota(jnp.int32, (1, CHUNK, 1), 1)
    prev = jnp.where(frame == 0, carry_ref[...], prev)
    carry_ref[...] = ps[:, CHUNK - 1:CHUNK, :]
    ps = (ps + (prev - ps) * mu_ref[...]).reshape(m_rows, D_SHIFT)

    r = ps[:, 0:D_RWKV]
    k = ps[:, D_RWKV:2 * D_RWKV]
    v = ps[:, 2 * D_RWKV:3 * D_RWKV]
    xl = ps[:, 3 * D_RWKV:D_SHIFT]

    lane = lax.broadcasted_iota(jnp.int32, (1, D_LORA), 1)
    lact = jnp.where(lane < LORA_W, jnp.tanh(xl),
                     jnp.where(lane < LORA_W + LORA_A, xl, jax.nn.sigmoid(xl)))
    lora = _dot(lact.astype(BF16), wlora_ref[...])
    lw = -DECAY_SCALE * jax.nn.sigmoid(vec(V_W0) + lora[:, 0:D_RWKV])
    a = jax.nn.sigmoid(vec(V_A0) + lora[:, D_RWKV:2 * D_RWKV])
    gate = lora[:, 2 * D_RWKV:3 * D_RWKV]

    kk = k * vec(V_KK)
    kk = kk * lax.rsqrt(jnp.maximum(_headsum(kk * kk, e_ref), 1e-12))
    k2 = k * (1.0 + (a - 1.0) * vec(V_KA))
    bonus = _headsum(r * k2 * vec(V_RK), e_ref) * v
    z = -kk
    b = kk * a

    row_t = lax.broadcasted_iota(jnp.int32, (CHUNK, GROUP), 0)
    col_s = lax.broadcasted_iota(jnp.int32, (CHUNK, GROUP), 1) % CHUNK
    strict = col_s < row_t
    incl = col_s <= row_t
    eye = (col_s == row_t).astype(F32)
    lt_r = lax.broadcasted_iota(jnp.int32, (CHUNK, CHUNK), 0)
    lt_c = lax.broadcasted_iota(jnp.int32, (CHUNK, CHUNK), 1)
    lower = (lt_c <= lt_r).astype(BF16)

    half_id = lax.broadcasted_iota(jnp.int32, (CHUNK, LANE_TILE), 1) // HEAD
    in_half = [half_id == 0, half_id == 1]
    half_keep = [m.astype(BF16) for m in in_half]
    tile_of = lambda hd: slice(LANE_TILE * (hd // 2), LANE_TILE * (hd // 2 + 1))
    zero_tile = jnp.zeros((CHUNK, LANE_TILE), BF16)

    def place(tiles):
        return jnp.concatenate(
            [jnp.concatenate([tiles[hd] if j == hd // 2 else zero_tile
                              for j in range(GROUP // LANE_TILE)], axis=1)
             for hd in range(HEADS_PER_GROUP)], axis=0)

    def bd(t):
        tb = t.astype(BF16)
        return place([tb[:, tile_of(hd)] * half_keep[hd % 2] for hd in range(HEADS_PER_GROUP)])

    chains = [(bi, g) for bi in range(n_batch) for g in range(N_GROUPS)]
    stage = lambda fn, *lists: [fn(*args) for args in zip(*lists)]

    def cumulate(bi):
        lw_b = lw[bi * CHUNK:(bi + 1) * CHUNK, :]
        hi, lo = _split2(lw_b)
        return _dot(lower, hi) + _dot(lower, lo)
    cums = [cumulate(bi) for bi in range(n_batch)]

    def prepare(chain):
        bi, g = chain
        rows = slice(bi * CHUNK, (bi + 1) * CHUNK)
        lanes = slice(g * GROUP, (g + 1) * GROUP)
        lw_g = lw[rows, lanes]
        cl = cums[bi][:, lanes]
        ref = cl[MID:MID + 1, :]
        e_out = jnp.exp(ref - cl)
        return dict(
            chain=chain,
            rt=(r[rows, lanes] * jnp.exp(cl - ref)).astype(BF16),
            zt=(z[rows, lanes] * jnp.exp(cl - lw_g - ref)).astype(BF16),
            kt=(k2[rows, lanes] * e_out).astype(BF16),
            bt=(b[rows, lanes] * e_out).astype(BF16),
            decay_in=jnp.exp(ref),
            decay_out=jnp.exp(cl[CHUNK - 1:CHUNK, :] - ref))

    def v_of(ci):
        bi, g = ci["chain"]
        return v[bi * CHUNK:(bi + 1) * CHUNK, g * GROUP:(g + 1) * GROUP]

    def load_state(ci):
        bi, g = ci["chain"]
        st = state_ref[bi, g]
        s0 = [st[HEAD * hd:HEAD * (hd + 1), :] * ci["decay_in"][:, tile_of(hd)]
              for hd in range(HEADS_PER_GROUP)]
        return s0, place([s.astype(BF16) for s in s0])

    def run_wave(wave):
        c = stage(prepare, wave)

        def a_matrices(ci):
            zr = jnp.concatenate([ci["zt"], ci["rt"]], axis=0)
            a_b = _dot_nt(zr, bd(ci["bt"]))
            a_k = _dot_nt(zr, bd(ci["kt"]))
            return dict(ci, a_bz=jnp.where(strict, a_b[:CHUNK], 0.0),
                        a_br=jnp.where(incl, a_b[CHUNK:], 0.0).astype(BF16),
                        a_kz=jnp.where(strict, a_k[:CHUNK], 0.0).astype(BF16),
                        a_kr=jnp.where(incl, a_k[CHUNK:], 0.0).astype(BF16))
        c = stage(a_matrices, c)

        t_acc = stage(lambda ci: eye + ci["a_bz"], c)
        pw = stage(lambda ci: _dot(ci["a_bz"].astype(BF16), bd(ci["a_bz"])), c)
        n = 2
        while n < CHUNK // 2:
            res = stage(lambda t, q: _dot(jnp.concatenate([t, q], axis=0).astype(BF16), bd(q)), t_acc, pw)
            t_acc = stage(lambda t, rr: t + rr[:CHUNK], t_acc, res)
            pw = stage(lambda rr: rr[CHUNK:], res)
            n *= 2
        t_acc = stage(lambda t, q: t + _dot(t.astype(BF16), bd(q)), t_acc, pw)

        s0 = stage(load_state, c)
        vbd = stage(lambda ci: bd(v_of(ci)), c)
        rhs0 = stage(lambda ci, s, vb: _dot_nt(ci["zt"], s[1]) + _dot(ci["a_kz"], vb), c, s0, vbd)
        u = stage(lambda t, q: _dot(t.astype(BF16), bd(q)), t_acc, rhs0)
        y = stage(lambda ci, s, ui, vb: _dot_nt(ci["rt"], s[1]) + _dot(ci["a_br"], bd(ui))
                  + _dot(ci["a_kr"], vb), c, s0, u, vbd)
        upd = stage(lambda ci, ui: _dot_tn(jnp.concatenate([ui, v_of(ci)], axis=0).astype(BF16),
                                           jnp.concatenate([ci["bt"], ci["kt"]], axis=0)), c, u)
        for ci, s, di in zip(c, s0, upd):
            bi, g = ci["chain"]
            state_ref[bi, g] = jnp.concatenate(
                [(s[0][hd] + jnp.where(in_half[hd % 2], di[HEAD * hd:HEAD * (hd + 1), tile_of(hd)], 0.0))
                 * ci["decay_out"][:, tile_of(hd)] for hd in range(HEADS_PER_GROUP)], axis=0)
        return y

    y = []
    for first in range(0, len(chains), WAVE):
        y += run_wave(chains[first:first + WAVE])

    y = jnp.concatenate([jnp.concatenate(y[N_GROUPS * bi:N_GROUPS * (bi + 1)], axis=1)
                         for bi in range(n_batch)], axis=0)
    mean = _headsum(y, e_ref, split=True) * (1.0 / HEAD)
    yc = y - mean
    var = _headsum(yc * yc, e_ref) * (1.0 / HEAD)
    yn = yc * lax.rsqrt(var + GN_EPS) * vec(V_GNW) + vec(V_GNB)
    out_a = (yn + bonus) * gate

    pc = p[:, D_SHIFT:]
    glu = pc[:, :D_CONV] * jax.nn.sigmoid(pc[:, D_CONV:])
    gh_ref[:, CONV_HIST:CONV_HIST + CHUNK, :] = glu.reshape(n_batch, CHUNK, D_CONV)
    conv_rows = []
    for bi in range(n_batch):
        gh = gh_ref[bi]
        acc = jnp.zeros((CHUNK, D_CONV), F32) + vec(V_CB)
        for rr in range(SUBLANES):
            shifted = pltpu.roll(gh, rr, axis=0) if rr else gh
            for q in range(CONV_HIST // SUBLANES):
                s = SUBLANES * q + rr
                if s >= CONV_WIDTH:
                    continue
                w = CONV_WIDTH - 1 - s
                lo = CONV_HIST - SUBLANES * q
                acc = acc + shifted[lo:lo + CHUNK, :] * dw_ref[w:w + 1, :]
        conv_rows.append(acc)
    gh_ref[:, 0:CONV_HIST, :] = gh_ref[:, CHUNK:CHUNK + CONV_HIST, :]
    cv = jnp.concatenate(conv_rows, axis=0)
    c_mean = jnp.mean(cv, axis=-1, keepdims=True)
    cc = cv - c_mean
    c_var = jnp.mean(cc * cc, axis=-1, keepdims=True)
    out_b = jax.nn.silu(cc * lax.rsqrt(c_var + LN_EPS) * vec(V_LNW) + vec(V_LNB))

    m = (_dot(out_a.astype(BF16), wout_ref[0:D_RWKV, :])
         + _dot(out_b.astype(BF16), wout_ref[D_RWKV:, :]))
    o_ref[...] = (x + _rmsnorm(m, gpost_ref[...])).reshape(n_batch, CHUNK, D_MODEL)


def _mixer(x, g_pre, g_post, w_in, mu, w_lora, vec, dw, e_mat, w_out):
    n_batch, seq, _ = x.shape
    const = lambda i: (0, 0)
    one = dict(pipeline_mode=pl.Buffered(1))
    tile = lambda i: (0, jnp.maximum(i - MIX_STAGE, 0), 0)
    chunk = lambda i: (jnp.minimum(i, MIX_STAGE - 1), 0)
    return pl.pallas_call(
        functools.partial(_mixer_kernel, n_batch=n_batch),
        out_shape=jax.ShapeDtypeStruct(x.shape, F32),
        grid=(MIX_STAGE + seq // CHUNK,),
        in_specs=[
            pl.BlockSpec((n_batch, CHUNK, D_MODEL), tile),
            pl.BlockSpec((1, D_MODEL), const),
            pl.BlockSpec((1, D_MODEL), const),
            pl.BlockSpec((MIX_ROWS, D_IN), chunk),
            pl.BlockSpec((1, D_SHIFT), const),
            pl.BlockSpec((D_LORA, 3 * D_RWKV), const, **one),
            pl.BlockSpec((N_VEC, D_RWKV), const),
            pl.BlockSpec((CONV_HIST, D_CONV), const),
            pl.BlockSpec((GROUP, GROUP), const, **one),
            pl.BlockSpec((MIX_ROWS, D_MODEL), chunk),
        ],
        out_specs=pl.BlockSpec((n_batch, CHUNK, D_MODEL), tile),
        scratch_shapes=[
            pltpu.VMEM((n_batch, 1, D_SHIFT), F32),
            pltpu.VMEM((n_batch, CONV_HIST + CHUNK, D_CONV), F32),
            pltpu.VMEM((n_batch, N_GROUPS, GROUP, LANE_TILE), F32),
            pltpu.VMEM((D_MODEL, D_IN), BF16),
            pltpu.VMEM((D_MODEL, D_MODEL), BF16),
        ],
        compiler_params=pltpu.CompilerParams(
            dimension_semantics=("arbitrary",), vmem_limit_bytes=VMEM_LIMIT),
        name="mixer",
    )(x, g_pre, g_post, w_in, mu, w_lora, vec, dw, e_mat, w_out)


def kernel(x, ffn1_norm_pre, ffn1_norm_post, ffn1_w_gu, ffn1_w_down, mix_norm_pre, mix_norm_post, w_in, shift_mu, w_up, w0, a_up, a0, g_up, k_k, k_a, r_k, gn_w, gn_b, conv_dw, conv_b, conv_ln_w, conv_ln_b, w_out, ffn2_norm_pre, ffn2_norm_post, ffn2_w_gu, ffn2_w_down):
    n_batch, seq, d_model = x.shape
    depth = ffn1_w_gu.shape[0]
    assert d_model == D_MODEL and seq % CHUNK == 0 and (n_batch * seq) % FFN_TM == 0

    head_id = jnp.arange(GROUP) // HEAD
    e_mat = (head_id[:, None] == head_id[None, :]).astype(BF16)

    for l in range(depth):
        w_lora = jnp.zeros((D_LORA, 3 * D_RWKV), F32)
        w_lora = w_lora.at[0:LORA_W, 0:D_RWKV].set(w_up[l])
        w_lora = w_lora.at[LORA_W:LORA_W + LORA_A, D_RWKV:2 * D_RWKV].set(a_up[l])
        w_lora = w_lora.at[LORA_W + LORA_A:, 2 * D_RWKV:].set(g_up[l])
        rows = [w0[l], a0[l], k_k[l], k_a[l], r_k[l].reshape(D_RWKV), gn_w[l], gn_b[l],
                conv_b[l], conv_ln_w[l], conv_ln_b[l]]
        vec = jnp.zeros((N_VEC, D_RWKV), F32).at[0:len(rows)].set(jnp.stack(rows))
        dw = jnp.zeros((CONV_HIST, D_CONV), F32).at[0:CONV_WIDTH].set(conv_dw[l])

        x2d = _ffn(x.reshape(n_batch * seq, D_MODEL),
                   ffn1_norm_pre[l].reshape(1, D_MODEL), ffn1_norm_post[l].reshape(1, D_MODEL),
                   ffn1_w_gu[l], ffn1_w_down[l])
        x = _mixer(x2d.reshape(n_batch, seq, D_MODEL),
                   mix_norm_pre[l].reshape(1, D_MODEL), mix_norm_post[l].reshape(1, D_MODEL),
                   w_in[l], shift_mu[l].reshape(1, D_SHIFT), w_lora.astype(BF16),
                   vec, dw, e_mat, w_out[l])
        x2d = _ffn(x.reshape(n_batch * seq, D_MODEL),
                   ffn2_norm_pre[l].reshape(1, D_MODEL), ffn2_norm_post[l].reshape(1, D_MODEL),
                   ffn2_w_gu[l], ffn2_w_down[l])
        x = x2d.reshape(n_batch, seq, D_MODEL)
    return x
```
